```python
import math
import jax
import jax.numpy as jnp
from jax import lax
import numpy as np

D_MODEL = 2048
BATCH = 4
SEQ = 2048
DEPTH = 4

MEM_LEN = 256
EPS = 1e-5
NEG_INF = -1e30
H_A = 8
D_A = 64
W_A = H_A * 2 * D_A
DILATED_GROUPS = ((128, 1), (512, 4), (2048, 16))
N_GROUPS = 3
HB_PER_GROUP = 4
D_B = 128
W_BQKV = N_GROUPS * HB_PER_GROUP * D_B
W_B = HB_PER_GROUP * D_B
H_C = 4
D_C = 128
W_C = H_C * D_C
IN_WIDTHS = (W_A, W_A, W_A, W_BQKV, W_BQKV, W_BQKV, W_C)
N_IN = 3 * W_A + 3 * W_BQKV + W_C
N_BRANCH = 3
NUM_BUCKETS = 32
REL_MAX_DISTANCE = 1024
N_BIAS_COLS = 2 * H_A + N_GROUPS * HB_PER_GROUP
Q_BLOCK = 128
D_FF = -(-8 * D_MODEL // (3 * 256)) * 256

kernel_name = "hybrid_diff_dilated_mem_encoder"


def rmsnorm(t, g):
    tf = t.astype(jnp.float32)
    tf = tf * lax.rsqrt(jnp.mean(tf * tf, axis=-1, keepdims=True) + EPS)
    return (tf * g.astype(jnp.float32)).astype(t.dtype)


def rel_bucket(rel):
    half_b = NUM_BUCKETS // 2
    max_exact = half_b // 2
    n = jnp.abs(rel)
    nf = jnp.maximum(n, 1).astype(jnp.float32)
    large = max_exact + (jnp.log(nf / max_exact) / math.log(REL_MAX_DISTANCE / max_exact)
                         * (half_b - max_exact)).astype(jnp.int32)
    large = jnp.minimum(large, half_b - 1)
    return jnp.where(rel > 0, half_b, 0) + jnp.where(n < max_exact, n, large)


def diff_attention(q, k, v, lam, lam_init, sub_gain, bias_tab):
    b, s = q.shape[0], q.shape[1]
    nq = s // Q_BLOCK
    scale = D_A ** -0.5
    kpos = jnp.arange(s)

    def one_block(args):
        qb, i0 = args
        rel = kpos[None, :] - (i0 + jnp.arange(Q_BLOCK))[:, None]
        bias = bias_tab[rel_bucket(rel)].reshape(Q_BLOCK, s, 2, H_A).transpose(3, 2, 0, 1)
        logits = jnp.einsum("bqhmd,bkhmd->bhmqk", qb, k).astype(jnp.float32) * scale + bias.astype(jnp.float32)
        p = jax.nn.softmax(logits, axis=-1)
        a = (p[:, :, 0] - lam * p[:, :, 1]).astype(v.dtype)
        return jnp.einsum("bhqk,bkhe->bqhe", a, v)

    qs = q.reshape(b, nq, Q_BLOCK, H_A, 2, D_A).transpose(1, 0, 2, 3, 4, 5)
    o = lax.map(one_block, (qs, jnp.arange(nq) * Q_BLOCK))
    o = o.transpose(1, 0, 2, 3, 4).reshape(b, s, H_A, 2 * D_A)
    o = rmsnorm(o, sub_gain) * (1.0 - lam_init)
    return o.reshape(b, s, W_A)


def dilated_group(q, k, v, bias_tab, window, dilation):
    b, s, hg, hd = q.shape
    half = window // (2 * dilation)
    L = s // dilation
    blk = half
    nb = -(-L // blk)
    lp = nb * blk

    def to_residue(t):
        t = t.reshape(b, L, dilation, hg, hd).transpose(0, 3, 2, 1, 4)
        return jnp.pad(t, ((0, 0), (0, 0), (0, 0), (0, lp - L), (0, 0)))

    def band(t):
        t = jnp.pad(t, ((0, 0), (0, 0), (0, 0), (blk, blk), (0, 0))).reshape(b, hg, dilation, nb + 2, blk, hd)
        return jnp.concatenate([t[:, :, :, :-2], t[:, :, :, 1:-1], t[:, :, :, 2:]], axis=4)

    qb = to_residue(q).reshape(b, hg, dilation, nb, blk, hd)
    kb = band(to_residue(k))
    vb = band(to_residue(v))
    kj = jnp.arange(3 * blk)[None, :] - blk
    step = kj - jnp.arange(blk)[:, None]
    bias = bias_tab[rel_bucket(step * dilation)].transpose(2, 0, 1)
    kidx = jnp.arange(nb)[:, None] * blk + kj
    valid = (jnp.abs(step) <= half)[None] & ((kidx >= 0) & (kidx < L))[:, None, :]
    logits = jnp.einsum("bhrnqd,bhrnkd->bhrnqk", qb, kb).astype(jnp.float32) * (hd ** -0.5) \
        + bias[None, :, None, None].astype(jnp.float32)
    logits = jnp.where(valid, logits, NEG_INF)
    lse = jax.nn.logsumexp(logits, axis=-1)
    p = jnp.exp(logits - lse[..., None]).astype(v.dtype)
    o = jnp.einsum("bhrnqk,bhrnkd->bhrnqd", p, vb)
    o = o.reshape(b, hg, dilation, lp, hd)[:, :, :, :L].transpose(0, 3, 2, 1, 4).reshape(b, s, hg, hd)
    lse = lse.reshape(b, hg, dilation, lp)[..., :L].transpose(0, 3, 2, 1).reshape(b, s, hg)
    return o, lse


def dilated_attention(q, k, v, bias_tab):
    b, s = q.shape[0], q.shape[1]
    q, k, v = (t.reshape(b, s, N_GROUPS, HB_PER_GROUP, D_B) for t in (q, k, v))
    outs, lses = [], []
    for g, (window, dilation) in enumerate(DILATED_GROUPS):
        cols = bias_tab[:, g * HB_PER_GROUP:(g + 1) * HB_PER_GROUP]
        o, lse = dilated_group(q[:, :, g], k[:, :, g], v[:, :, g], cols, window, dilation)
        outs.append(o)
        lses.append(lse)
    alpha = jax.nn.softmax(jnp.stack(lses), axis=0)
    o = jnp.sum(alpha[..., None] * jnp.stack(outs).astype(jnp.float32), axis=0)
    return o.astype(q.dtype).reshape(b, s, W_B)


def mem_attention(q, mem_kv):
    b, s = q.shape[0], q.shape[1]
    q = q.reshape(b, s, H_C, D_C)
    kv = mem_kv.reshape(b, mem_kv.shape[1], 2, H_C, D_C)
    logits = jnp.einsum("bshd,bmhd->bhsm", q, kv[:, :, 0]).astype(jnp.float32) * (D_C ** -0.5)
    p = jax.nn.softmax(logits, axis=-1).astype(q.dtype)
    return jnp.einsum("bhsm,bmhd->bshd", p, kv[:, :, 1]).reshape(b, s, W_C)


def setup_inputs(seed: int = 0) -> dict:
    key = jax.random.key(seed)
    ks = jax.random.split(key, 20)
    f32 = jnp.float32

    def nrm(k, shape, scale):
        return jax.random.normal(k, shape, f32) * scale

    def gain(k, shape):
        return 1.0 + 0.05 * jax.random.normal(k, shape, f32)

    return {
        "x": nrm(ks[0], (BATCH, SEQ, D_MODEL), 1.0),
        "mem": nrm(ks[1], (BATCH, MEM_LEN, D_MODEL), 1.0),
        "rel_bias": nrm(ks[2], (NUM_BUCKETS, N_BIAS_COLS), 0.5),
        "mem_norm": gain(ks[3], (D_MODEL,)),
        "attn_norm": gain(ks[4], (DEPTH, D_MODEL)),
        "w_in": nrm(ks[5], (DEPTH, D_MODEL, N_IN), D_MODEL ** -0.5),
        "diff_lambda": nrm(ks[6], (DEPTH, 4, D_A), 0.1),
        "diff_subln": gain(ks[7], (DEPTH, 2 * D_A)),
        "w_mem_kv": nrm(ks[8], (DEPTH, D_MODEL, 2 * W_C), D_MODEL ** -0.5),
        "w_gate": nrm(ks[9], (DEPTH, D_MODEL, N_BRANCH * D_MODEL), D_MODEL ** -0.5),
        "b_gate": nrm(ks[10], (DEPTH, N_BRANCH * D_MODEL), 0.1),
        "w_proj_a": nrm(ks[11], (DEPTH, W_A, D_MODEL), W_A ** -0.5),
        "w_proj_b": nrm(ks[12], (DEPTH, W_B, D_MODEL), W_B ** -0.5),
        "w_proj_c": nrm(ks[13], (DEPTH, W_C, D_MODEL), W_C ** -0.5),
        "w_out": nrm(ks[14], (DEPTH, D_MODEL, D_MODEL), D_MODEL ** -0.5),
        "ffn_norm": gain(ks[15], (DEPTH, D_MODEL)),
        "w_ffn_gate": nrm(ks[16], (DEPTH, D_MODEL, D_FF), D_MODEL ** -0.5),
        "w_ffn_up": nrm(ks[17], (DEPTH, D_MODEL, D_FF), D_MODEL ** -0.5),
        "w_ffn_down": nrm(ks[18], (DEPTH, D_FF, D_MODEL), D_FF ** -0.5),
        "final_norm": gain(ks[19], (D_MODEL,)),
    }


def reference(x, mem, rel_bias, mem_norm, attn_norm, w_in, diff_lambda, diff_subln, w_mem_kv,
              w_gate, b_gate, w_proj_a, w_proj_b, w_proj_c, w_out, ffn_norm, w_ffn_gate,
              w_ffn_up, w_ffn_down, final_norm):
    b, s, _ = x.shape
    split_at = np.cumsum(IN_WIDTHS)[:-1].tolist()
    mem_n = rmsnorm(mem, mem_norm)
    bias_a = rel_bias[:, :2 * H_A]
    bias_b = rel_bias[:, 2 * H_A:]
    for l in range(DEPTH):
        h = rmsnorm(x, attn_norm[l])
        qa, ka, va, qb, kb, vb, qc = jnp.split(h @ w_in[l], split_at, axis=-1)
        dl = diff_lambda[l].astype(jnp.float32)
        lam_init = 0.8 - 0.6 * math.exp(-0.3 * l)
        lam = jnp.exp(jnp.sum(dl[0] * dl[1])) - jnp.exp(jnp.sum(dl[2] * dl[3])) + lam_init
        o_a = diff_attention(qa.reshape(b, s, H_A, 2, D_A), ka.reshape(b, s, H_A, 2, D_A),
                             va.reshape(b, s, H_A, 2 * D_A), lam, lam_init, diff_subln[l], bias_a)
        o_b = dilated_attention(qb, kb, vb, bias_b)
        o_c = mem_attention(qc, mem_n @ w_mem_kv[l])
        gates = jax.nn.sigmoid((h @ w_gate[l] + b_gate[l]).astype(jnp.float32)).astype(h.dtype)
        gates = gates.reshape(b, s, N_BRANCH, D_MODEL)
        merged = (gates[:, :, 0] * (o_a @ w_proj_a[l])
                  + gates[:, :, 1] * (o_b @ w_proj_b[l])
                  + gates[:, :, 2] * (o_c @ w_proj_c[l]))
        x = x + merged @ w_out[l]
        h2 = rmsnorm(x, ffn_norm[l])
        x = x + (jax.nn.silu(h2 @ w_ffn_gate[l]) * (h2 @ w_ffn_up[l])) @ w_ffn_down[l]
    return rmsnorm(x, final_norm)
```

```python
import functools
import math

import jax
import jax.numpy as jnp
from jax import lax
from jax.experimental import pallas as pl
from jax.experimental.pallas import tpu as pltpu

F32 = jnp.float32
BF16 = jnp.bfloat16

D_MODEL = 2048
DEPTH = 4
MEM_LEN = 256
EPS = 1e-5
NEG_INF = -1e30
H_A = 8
D_A = 64
W_A = H_A * 2 * D_A
DILATED_GROUPS = ((128, 1), (512, 4), (2048, 16))
N_GROUPS = 3
HB = 4
D_B = 128
W_BQKV = N_GROUPS * HB * D_B
W_B = HB * D_B
H_C = 4
D_C = 128
W_C = H_C * D_C
N_IN = 3 * W_A + 3 * W_BQKV + W_C
NUM_BUCKETS = 32
REL_MAX_DISTANCE = 1024
N_BIAS_COLS = 2 * H_A + N_GROUPS * HB
D_FF = 5632

OFF_QA, OFF_KA, OFF_VA = 0, W_A, 2 * W_A
OFF_QB = 3 * W_A
OFF_KB = OFF_QB + W_BQKV
OFF_VB = OFF_KB + W_BQKV
OFF_QC = OFF_VB + W_BQKV

TQ = 128
TL = 128
HALF = 64
VMEM_LIMIT = 56 * 1024 * 1024

_NT = (((1,), (1,)), ((), ()))


def _rms(xf, g):
    return xf * lax.rsqrt(jnp.mean(xf * xf, axis=-1, keepdims=True) + EPS) * g


def _dot(a, b):
    return jnp.dot(a, b, preferred_element_type=F32)


def _cparams(sem):
    return pltpu.CompilerParams(dimension_semantics=sem, vmem_limit_bytes=VMEM_LIMIT)


def _rel_bucket(rel):
    half_b = NUM_BUCKETS // 2
    max_exact = half_b // 2
    n = jnp.abs(rel)
    nf = jnp.maximum(n, 1).astype(F32)
    large = max_exact + (jnp.log(nf / max_exact) / math.log(REL_MAX_DISTANCE / max_exact)
                         * (half_b - max_exact)).astype(jnp.int32)
    large = jnp.minimum(large, half_b - 1)
    return jnp.where(rel > 0, half_b, 0) + jnp.where(n < max_exact, n, large)


def _lookup(bucket, tab_ref, col):
    out = jnp.zeros(bucket.shape, F32)
    for b in range(NUM_BUCKETS):
        out = jnp.where(bucket == b, tab_ref[b, col], out)
    return out


def _bias_a_kernel(tab_ref, e_ref, *, nq):
    j = pl.program_id(0)
    r = lax.broadcasted_iota(jnp.int32, (TQ, TQ), 0)
    kk = lax.broadcasted_iota(jnp.int32, (TQ, TQ), 1)
    bucket = _rel_bucket((j - (nq - 1)) * TQ + kk - r)
    for m in range(2):
        for h in range(H_A):
            e_ref[h, m] = _lookup(bucket, tab_ref, m * H_A + h)


def _bias_a(rel_bias, nq):
    nt = 2 * nq - 1
    return pl.pallas_call(
        functools.partial(_bias_a_kernel, nq=nq),
        out_shape=jax.ShapeDtypeStruct((H_A, 2, nt, TQ, TQ), F32),
        grid=(nt,),
        in_specs=[pl.BlockSpec(memory_space=pltpu.SMEM)],
        out_specs=pl.BlockSpec((H_A, 2, None, TQ, TQ), lambda j: (0, 0, j, 0, 0)),
        compiler_params=_cparams(("parallel",)),
        name="bias_a",
    )(rel_bias)


def _bias_b_kernel(tab_ref, t_ref):
    g = pl.program_id(0)
    r = lax.broadcasted_iota(jnp.int32, (TL, 2 * TL), 0)
    kk = lax.broadcasted_iota(jnp.int32, (TL, 2 * TL), 1)
    step = kk - HALF - r
    dil = jnp.where(g == 0, DILATED_GROUPS[0][1],
                    jnp.where(g == 1, DILATED_GROUPS[1][1], DILATED_GROUPS[2][1]))
    bucket = _rel_bucket(step * dil)
    for hh in range(HB):
        t_ref[hh] = _lookup(bucket, tab_ref, 2 * H_A + g * HB + hh)


def _bias_b(rel_bias):
    return pl.pallas_call(
        _bias_b_kernel,
        out_shape=jax.ShapeDtypeStruct((N_GROUPS, HB, TL, 2 * TL), F32),
        grid=(N_GROUPS,),
        in_specs=[pl.BlockSpec(memory_space=pltpu.SMEM)],
        out_specs=pl.BlockSpec((None, HB, TL, 2 * TL), lambda g: (g, 0, 0, 0)),
        compiler_params=_cparams(("parallel",)),
        name="bias_b",
    )(rel_bias)


def _rms_matmul_kernel(x_ref, g_ref, w_ref, o_ref, h_scr):
    @pl.when(pl.program_id(1) == 0)
    def _():
        h_scr[...] = _rms(x_ref[...], g_ref[...]).astype(BF16)

    o_ref[...] = _dot(h_scr[...], w_ref[...]).astype(o_ref.dtype)


def _rms_matmul(x, gains, w, l, lg, *, tm, tn):
    m, k = x.shape
    n = w.shape[-1]
    return pl.pallas_call(
        _rms_matmul_kernel,
        out_shape=jax.ShapeDtypeStruct((m, n), BF16),
        grid=(m // tm, n // tn),
        in_specs=[
            pl.BlockSpec((tm, k), lambda i, j: (i, 0)),
            pl.BlockSpec((None, 1, k), lambda i, j: (lg, 0, 0)),
            pl.BlockSpec((None, k, tn), lambda i, j: (l, 0, j)),
        ],
        out_specs=pl.BlockSpec((tm, tn), lambda i, j: (i, j)),
        scratch_shapes=[pltpu.VMEM((tm, k), BF16)],
        compiler_params=_cparams(("parallel", "arbitrary")),
        name="rms_matmul",
    )(x, gains, w)


def _attn_a_kernel(dl_ref, q_ref, k_ref, v_ref, e_ref, sg_ref, o_ref, *, lam_init, nq):
    i = pl.program_id(2)
    dl = dl_ref[...]
    lam = (jnp.exp(jnp.sum(dl[0:1] * dl[1:2], axis=-1, keepdims=True))
           - jnp.exp(jnp.sum(dl[2:3] * dl[3:4], axis=-1, keepdims=True)) + lam_init)

    q = q_ref[...] * (D_A ** -0.5)
    lane = lax.broadcasted_iota(jnp.int32, q.shape, 1)
    zero = jnp.zeros_like(q)
    qq = jnp.concatenate([jnp.where(lane < D_A, q, zero), jnp.where(lane >= D_A, q, zero)], axis=0)
    s = lax.dot_general(qq, k_ref[...], _NT, preferred_element_type=F32)

    bias = [jnp.concatenate([e_ref[m, nq - 1 - i + c] for c in range(nq)], axis=1) for m in range(2)]
    s = s + jnp.concatenate(bias, axis=0)
    e = jnp.exp(s - jnp.max(s, axis=-1, keepdims=True))
    rinv = 1.0 / jnp.sum(e, axis=-1, keepdims=True)
    a = e[:TQ] * rinv[:TQ] - e[TQ:] * (lam * rinv[TQ:])
    o = _dot(a.astype(BF16), v_ref[...])
    o_ref[...] = (_rms(o, sg_ref[...]) * (1.0 - lam_init)).astype(o_ref.dtype)


def _attn_a(qkv, diff_lambda, subln, e_a, l, lam_init):
    b, s, _ = qkv.shape
    nq = s // TQ
    w = 2 * D_A
    return pl.pallas_call(
        functools.partial(_attn_a_kernel, lam_init=lam_init, nq=nq),
        out_shape=jax.ShapeDtypeStruct((b, s, W_A), BF16),
        grid=(H_A, b, nq),
        in_specs=[
            pl.BlockSpec((None, 4, D_A), lambda h, bb, i: (l, 0, 0)),
            pl.BlockSpec((None, TQ, w), lambda h, bb, i: (bb, i, OFF_QA // w + h)),
            pl.BlockSpec((None, s, w), lambda h, bb, i: (bb, 0, OFF_KA // w + h)),
            pl.BlockSpec((None, s, w), lambda h, bb, i: (bb, 0, OFF_VA // w + h)),
            pl.BlockSpec((None, 2, 2 * nq - 1, TQ, TQ), lambda h, bb, i: (h, 0, 0, 0, 0)),
            pl.BlockSpec((None, 1, w), lambda h, bb, i: (l, 0, 0)),
        ],
        out_specs=pl.BlockSpec((None, TQ, w), lambda h, bb, i: (bb, i, h)),
        compiler_params=_cparams(("parallel", "parallel", "parallel")),
        name="attn_a",
    )(diff_lambda, qkv, qkv, qkv, e_a, subln)


def _attn_b_kernel(*refs, cls_len, has_prev, is_last):
    if has_prev:
        q_ref, k_ref, v_ref, t_ref, op_ref, lp_ref = refs[:6]
        outs = refs[6:]
    else:
        q_ref, k_ref, v_ref, t_ref = refs[:4]
        outs = refs[4:]
    o_ref = outs[0]
    it = pl.program_id(2)
    q0 = pl.multiple_of(it * TL, TL)
    left = pl.multiple_of(lax.rem(q0 + (cls_len - HALF), cls_len), HALF)
    right = pl.multiple_of(lax.rem(q0 + TL, cls_len), HALF)

    r = lax.broadcasted_iota(jnp.int32, (TL, 2 * TL), 0)
    kk = lax.broadcasted_iota(jnp.int32, (TL, 2 * TL), 1)
    kpos = q0 - HALF + kk
    valid = (jnp.abs(kk - HALF - r) <= HALF) & (kpos >= 0) & (kpos < cls_len)

    def window(ref, cs):
        return jnp.concatenate([ref[pl.ds(left, HALF), cs], ref[pl.ds(q0, TL), cs],
                                ref[pl.ds(right, HALF), cs]], axis=0)

    for hh in range(HB):
        cs = slice(hh * D_B, (hh + 1) * D_B)
        s = lax.dot_general(q_ref[:, cs], window(k_ref, cs), _NT, preferred_element_type=F32)
        s = jnp.where(valid, s * (D_B ** -0.5) + t_ref[hh], NEG_INF)
        mx = jnp.max(s, axis=-1, keepdims=True)
        e = jnp.exp(s - mx)
        den = jnp.sum(e, axis=-1, keepdims=True)
        o = _dot(e.astype(BF16), window(v_ref, cs)) / den
        lse = jnp.broadcast_to(mx + jnp.log(den), o.shape)
        if has_prev:
            lp = lp_ref[:, cs]
            top = jnp.maximum(lp, lse)
            wp = jnp.exp(lp - top)
            wn = jnp.exp(lse - top)
            o = (op_ref[:, cs] * wp + o * wn) / (wp + wn)
            lse = top + jnp.log(wp + wn)
        o_ref[:, cs] = o.astype(o_ref.dtype)
        if not is_last:
            outs[1][:, cs] = lse


def _attn_b_group(qkv, t_b, g, prev):
    b, s, n_in = qkv.shape
    _, dil = DILATED_GROUPS[g]
    cls_len = s // dil
    has_prev = prev is not None
    is_last = g == N_GROUPS - 1
    cb = n_in // W_B

    def view(t):
        return t.reshape(b, cls_len, dil * t.shape[-1])

    tok = lambda bb, rr, it: (bb, it, rr)
    in_specs = [
        pl.BlockSpec((None, TL, W_B), lambda bb, rr, it: (bb, it, rr * cb + OFF_QB // W_B + g)),
        pl.BlockSpec((None, cls_len, W_B), lambda bb, rr, it: (bb, 0, rr * cb + OFF_KB // W_B + g)),
        pl.BlockSpec((None, cls_len, W_B), lambda bb, rr, it: (bb, 0, rr * cb + OFF_VB // W_B + g)),
        pl.BlockSpec((None, HB, TL, 2 * TL), lambda bb, rr, it: (g, 0, 0, 0)),
    ]
    args = [view(qkv), view(qkv), view(qkv), t_b]
    if has_prev:
        in_specs += [pl.BlockSpec((None, TL, W_B), tok)] * 2
        args += [view(prev[0]), view(prev[1])]
    o_dtype = BF16 if is_last else F32
    out_shape = [jax.ShapeDtypeStruct((b, cls_len, dil * W_B), o_dtype)]
    out_specs = [pl.BlockSpec((None, TL, W_B), tok)]
    if not is_last:
        out_shape.append(jax.ShapeDtypeStruct((b, cls_len, dil * W_B), F32))
        out_specs.append(pl.BlockSpec((None, TL, W_B), tok))
    outs = pl.pallas_call(
        functools.partial(_attn_b_kernel, cls_len=cls_len, has_prev=has_prev, is_last=is_last),
        out_shape=out_shape,
        grid=(b, dil, cls_len // TL),
        in_specs=in_specs,
        out_specs=out_specs,
        compiler_params=_cparams(("parallel", "parallel", "parallel")),
        name=f"attn_b{g}",
    )(*args)
    return [o.reshape(b, s, W_B) for o in outs]


def _attn_b(qkv, t_b):
    prev = None
    for g in range(N_GROUPS):
        prev = _attn_b_group(qkv, t_b, g, prev)
    return prev[0]


def _merge_kernel(x_ref, g_ref, oa_ref, ob_ref, qc_ref, kv_ref, wg0_ref, wg1_ref, wg2_ref, bg_ref,
                  wpa_ref, wpb_ref, wpc_ref, o_ref, h_scr, oc_scr):
    @pl.when(pl.program_id(1) == 0)
    def _():
        h_scr[...] = _rms(x_ref[...], g_ref[...]).astype(BF16)
        for hh in range(H_C):
            cs = slice(hh * D_C, (hh + 1) * D_C)
            vs = slice(W_C + hh * D_C, W_C + (hh + 1) * D_C)
            s = lax.dot_general(qc_ref[:, cs], kv_ref[:, cs], _NT, preferred_element_type=F32) * (D_C ** -0.5)
            e = jnp.exp(s - jnp.max(s, axis=-1, keepdims=True))
            p = e / jnp.sum(e, axis=-1, keepdims=True)
            oc_scr[:, cs] = _dot(p.astype(BF16), kv_ref[:, vs]).astype(BF16)

    h = h_scr[...]

    def gate(w_ref, br):
        return jax.nn.sigmoid(_dot(h, w_ref[...]) + bg_ref[br])

    merged = (gate(wg0_ref, 0) * _dot(oa_ref[...], wpa_ref[...])
              + gate(wg1_ref, 1) * _dot(ob_ref[...], wpb_ref[...])
              + gate(wg2_ref, 2) * _dot(oc_scr[...], wpc_ref[...]))
    o_ref[...] = merged.astype(o_ref.dtype)


def _merge(x, gains, o_a, o_b, qkv, mem_kv, w_gate, b_gate, w_pa, w_pb, w_pc, l, seq, *, tm, tn):
    m, d = x.shape
    nd = d // tn
    per_batch = seq // tm

    def wg_spec(br):
        return pl.BlockSpec((None, d, tn), lambda i, j: (l, 0, br * nd + j))

    return pl.pallas_call(
        _merge_kernel,
        out_shape=jax.ShapeDtypeStruct((m, d), BF16),
        grid=(m // tm, nd),
        in_specs=[
            pl.BlockSpec((tm, d), lambda i, j: (i, 0)),
            pl.BlockSpec((None, 1, d), lambda i, j: (l, 0, 0)),
            pl.BlockSpec((tm, W_A), lambda i, j: (i, 0)),
            pl.BlockSpec((tm, W_B), lambda i, j: (i, 0)),
            pl.BlockSpec((tm, W_C), lambda i, j: (i, OFF_QC // W_C)),
            pl.BlockSpec((MEM_LEN, 2 * W_C), lambda i, j: (i // per_batch, 0)),
            wg_spec(0), wg_spec(1), wg_spec(2),
            pl.BlockSpec((None, 3, 1, tn), lambda i, j: (l, 0, 0, j)),
            pl.BlockSpec((None, W_A, tn), lambda i, j: (l, 0, j)),
            pl.BlockSpec((None, W_B, tn), lambda i, j: (l, 0, j)),
            pl.BlockSpec((None, W_C, tn), lambda i, j: (l, 0, j)),
        ],
        out_specs=pl.BlockSpec((tm, tn), lambda i, j: (i, j)),
        scratch_shapes=[pltpu.VMEM((tm, d), BF16), pltpu.VMEM((tm, W_C), BF16)],
        compiler_params=_cparams(("parallel", "arbitrary")),
        name="merge",
    )(x, gains, o_a, o_b, qkv, mem_kv, w_gate, w_gate, w_gate, b_gate, w_pa, w_pb, w_pc)


def _out_proj_kernel(x_ref, m_ref, w_ref, o_ref):
    o_ref[...] = x_ref[...] + _dot(m_ref[...], w_ref[...])


def _out_proj(x, merged, w_out, l, *, tm, tn):
    m, d = x.shape
    return pl.pallas_call(
        _out_proj_kernel,
        out_shape=jax.ShapeDtypeStruct((m, d), F32),
        grid=(m // tm, d // tn),
        in_specs=[
            pl.BlockSpec((tm, tn), lambda i, j: (i, j)),
            pl.BlockSpec((tm, d), lambda i, j: (i, 0)),
            pl.BlockSpec((None, d, tn), lambda i, j: (l, 0, j)),
        ],
        out_specs=pl.BlockSpec((tm, tn), lambda i, j: (i, j)),
        compiler_params=_cparams(("parallel", "parallel")),
        name="out_proj",
    )(x, merged, w_out)


def _ffn_kernel(x_ref, g_ref, wg_ref, wu_ref, wd_ref, fg_ref, o_ref, h_scr, acc_scr, *, final):
    j = pl.program_id(1)

    @pl.when(j == 0)
    def _():
        h_scr[...] = _rms(x_ref[...], g_ref[...]).astype(BF16)
        acc_scr[...] = jnp.zeros_like(acc_scr)

    h = h_scr[...]
    gt = _dot(h, wg_ref[...])
    act = gt * jax.nn.sigmoid(gt) * _dot(h, wu_ref[...])
    acc_scr[...] += _dot(act.astype(BF16), wd_ref[...])

    @pl.when(j == pl.num_programs(1) - 1)
    def _():
        y = x_ref[...] + acc_scr[...]
        if final:
            y = _rms(y, fg_ref[...])
        o_ref[...] = y


def _ffn(x, gains, w_g, w_u, w_d, final_gain, l, final, *, tm, tf):
    m, d = x.shape
    f = w_g.shape[-1]
    return pl.pallas_call(
        functools.partial(_ffn_kernel, final=final),
        out_shape=jax.ShapeDtypeStruct((m, d), F32),
        grid=(m // tm, f // tf),
        in_specs=[
            pl.BlockSpec((tm, d), lambda i, j: (i, 0)),
            pl.BlockSpec((None, 1, d), lambda i, j: (l, 0, 0)),
            pl.BlockSpec((None, d, tf), lambda i, j: (l, 0, j)),
            pl.BlockSpec((None, d, tf), lambda i, j: (l, 0, j)),
            pl.BlockSpec((None, tf, d), lambda i, j: (l, j, 0)),
            pl.BlockSpec((1, d), lambda i, j: (0, 0)),
        ],
        out_specs=pl.BlockSpec((tm, d), lambda i, j: (i, 0)),
        scratch_shapes=[pltpu.VMEM((tm, d), BF16), pltpu.VMEM((tm, d), F32)],
        compiler_params=_cparams(("parallel", "arbitrary")),
        name="ffn",
    )(x, gains, w_g, w_u, w_d, final_gain)


def kernel(x, mem, rel_bias, mem_norm, attn_norm, w_in, diff_lambda, diff_subln, w_mem_kv, w_gate, b_gate,
           w_proj_a, w_proj_b, w_proj_c, w_out, ffn_norm, w_ffn_gate, w_ffn_up, w_ffn_down, final_norm):
    b, s, d = x.shape
    depth = w_in.shape[0]
    assert d == D_MODEL and s % TQ == 0 and all(s % (dil * TL) == 0 for _, dil in DILATED_GROUPS)
    xf = x.reshape(b * s, d)
    memf = mem.reshape(b * MEM_LEN, d)

    w_in, w_mem_kv, w_gate, w_proj_a, w_proj_b, w_proj_c, w_out, w_ffn_gate, w_ffn_up, w_ffn_down = (
        t.astype(BF16) for t in (w_in, w_mem_kv, w_gate, w_proj_a, w_proj_b, w_proj_c, w_out,
                                 w_ffn_gate, w_ffn_up, w_ffn_down))
    attn_norm = attn_norm.reshape(depth, 1, d)
    ffn_norm = ffn_norm.reshape(depth, 1, d)
    mem_norm = mem_norm.reshape(1, 1, d)
    final_norm = final_norm.reshape(1, d)
    subln = diff_subln.reshape(depth, 1, 2 * D_A)
    b_gate = b_gate.reshape(depth, 3, 1, d)

    e_a = _bias_a(rel_bias, s // TQ)
    t_b = _bias_b(rel_bias)

    for l in range(depth):
        lam_init = 0.8 - 0.6 * math.exp(-0.3 * l)
        qkv = _rms_matmul(xf, attn_norm, w_in, l, l, tm=1024, tn=1024).reshape(b, s, N_IN)
        mem_kv = _rms_matmul(memf, mem_norm, w_mem_kv, l, 0, tm=512, tn=1024)
        o_a = _attn_a(qkv, diff_lambda, subln, e_a, l, lam_init).reshape(b * s, W_A)
        o_b = _attn_b(qkv, t_b).reshape(b * s, W_B)
        merged = _merge(xf, attn_norm, o_a, o_b, qkv.reshape(b * s, N_IN), mem_kv, w_gate, b_gate,
                        w_proj_a, w_proj_b, w_proj_c, l, s, tm=512, tn=512)
        x1 = _out_proj(xf, merged, w_out, l, tm=1024, tn=1024)
        xf = _ffn(x1, ffn_norm, w_ffn_gate, w_ffn_up, w_ffn_down, final_norm, l, l == depth - 1,
                  tm=512, tf=512)
    return xf.reshape(b, s, d)
```

```python
import functools
import math

import jax
import jax.numpy as jnp
from jax import lax
from jax.experimental import pallas as pl
from jax.experimental.pallas import tpu as pltpu

F32 = jnp.float32
BF16 = jnp.bfloat16

D_MODEL = 2048
MEM_LEN = 256
EPS = 1e-5
NEG_INF = -1e30
H_A = 8
D_A = 64
W_A = H_A * 2 * D_A
DILATED_GROUPS = ((128, 1), (512, 4), (2048, 16))
N_GROUPS = 3
HB = 4
D_B = 128
W_BQKV = N_GROUPS * HB * D_B
W_B = HB * D_B
H_C = 4
D_C = 128
W_C = H_C * D_C
N_IN = 3 * W_A + 3 * W_BQKV + W_C
NUM_BUCKETS = 32
REL_MAX_DISTANCE = 1024

TQ = 128
TL = 128
HALF = 64
LANES = 128
VMEM_LIMIT = 56 * 1024 * 1024

N_TILES_A = 3 * W_A // W_B
NAT_TILES = tuple(range(N_TILES_A)) + tuple(N_TILES_A + 3 * t for t in range(4))
STR_TILES = tuple(N_TILES_A + 3 * t + g for t in range(3) for g in (1, 2))
NAT_QB0, NAT_KB0, NAT_VB0, NAT_QC = (N_TILES_A + t for t in range(4))

_NT = (((1,), (1,)), ((), ()))


def _rms(xf, g):
    return xf * lax.rsqrt(jnp.mean(xf * xf, axis=-1, keepdims=True) + EPS) * g


def _dot(a, b):
    return jnp.dot(a, b, preferred_element_type=F32)


def _cparams(sem):
    return pltpu.CompilerParams(dimension_semantics=sem, vmem_limit_bytes=VMEM_LIMIT)


def _rel_bucket(rel):
    half_b = NUM_BUCKETS // 2
    max_exact = half_b // 2
    n = jnp.abs(rel)
    nf = jnp.maximum(n, 1).astype(F32)
    large = max_exact + (jnp.log(nf / max_exact) / math.log(REL_MAX_DISTANCE / max_exact)
                         * (half_b - max_exact)).astype(jnp.int32)
    large = jnp.minimum(large, half_b - 1)
    return jnp.where(rel > 0, half_b, 0) + jnp.where(n < max_exact, n, large)


def _lookup(bucket, tab_ref, col):
    out = jnp.zeros(bucket.shape, F32)
    for b in range(NUM_BUCKETS):
        out = jnp.where(bucket == b, tab_ref[b, col], out)
    return out


def _bias_a_kernel(tab_ref, e_ref, *, nq):
    j = pl.program_id(0)
    r = lax.broadcasted_iota(jnp.int32, (TQ, TQ), 0)
    kk = lax.broadcasted_iota(jnp.int32, (TQ, TQ), 1)
    bucket = _rel_bucket((j - (nq - 1)) * TQ + kk - r)
    for m in range(2):
        for h in range(H_A):
            e_ref[h, m] = _lookup(bucket, tab_ref, m * H_A + h)


def _bias_a(rel_bias, nq):
    nt = 2 * nq - 1
    return pl.pallas_call(
        functools.partial(_bias_a_kernel, nq=nq),
        out_shape=jax.ShapeDtypeStruct((H_A, 2, nt, TQ, TQ), F32),
        grid=(nt,),
        in_specs=[pl.BlockSpec(memory_space=pltpu.SMEM)],
        out_specs=pl.BlockSpec((H_A, 2, None, TQ, TQ), lambda j: (0, 0, j, 0, 0)),
        compiler_params=_cparams(("parallel",)),
        name="bias_a",
    )(rel_bias)


def _halo(cls_len):
    return 0 if cls_len == TL else HALF


def _bias_b_kernel(tab_ref, t_ref, *, seq):
    g = pl.program_id(0)
    r = lax.broadcasted_iota(jnp.int32, (TL, 2 * TL), 0)
    kk = lax.broadcasted_iota(jnp.int32, (TL, 2 * TL), 1)
    halo, dil = _halo(seq // DILATED_GROUPS[0][1]), DILATED_GROUPS[0][1]
    for gg in range(1, N_GROUPS):
        halo = jnp.where(g == gg, _halo(seq // DILATED_GROUPS[gg][1]), halo)
        dil = jnp.where(g == gg, DILATED_GROUPS[gg][1], dil)
    step = kk - halo - r
    bucket = _rel_bucket(step * dil)
    for hh in range(HB):
        t_ref[hh] = jnp.where(jnp.abs(step) <= HALF, _lookup(bucket, tab_ref, 2 * H_A + g * HB + hh), NEG_INF)


def _bias_b(rel_bias, seq):
    return pl.pallas_call(
        functools.partial(_bias_b_kernel, seq=seq),
        out_shape=jax.ShapeDtypeStruct((N_GROUPS, HB, TL, 2 * TL), F32),
        grid=(N_GROUPS,),
        in_specs=[pl.BlockSpec(memory_space=pltpu.SMEM)],
        out_specs=pl.BlockSpec((None, HB, TL, 2 * TL), lambda g: (g, 0, 0, 0)),
        compiler_params=_cparams(("parallel",)),
        name="bias_b",
    )(rel_bias)


def _rms_matmul_kernel(x_ref, g_ref, w_ref, o_ref, h_scr):
    @pl.when(pl.program_id(1) == 0)
    def _():
        h_scr[...] = _rms(x_ref[...], g_ref[...]).astype(BF16)

    o_ref[...] = _dot(h_scr[...], w_ref[...]).astype(o_ref.dtype)


def _tile_lookup(tiles):
    def f(j):
        out = tiles[0]
        for idx, t in enumerate(tiles[1:], 1):
            out = jnp.where(j == idx, t, out)
        return out
    return f


def _rms_matmul(x, gains, w, l, lg, *, tm, tn, n_out, col_tile, out_dtype):
    m, k = x.shape
    return pl.pallas_call(
        _rms_matmul_kernel,
        out_shape=jax.ShapeDtypeStruct((m, n_out), out_dtype),
        grid=(m // tm, n_out // tn),
        in_specs=[
            pl.BlockSpec((tm, k), lambda i, j: (i, 0)),
            pl.BlockSpec((None, 1, k), lambda i, j: (lg, 0, 0)),
            pl.BlockSpec((None, k, tn), lambda i, j: (l, 0, col_tile(j))),
        ],
        out_specs=pl.BlockSpec((tm, tn), lambda i, j: (i, j)),
        scratch_shapes=[pltpu.VMEM((tm, k), BF16)],
        compiler_params=_cparams(("parallel", "arbitrary")),
        name="rms_matmul",
    )(x, gains, w)


def _attn_a_kernel(dl_ref, q_ref, k_ref, v_ref, e_ref, sg_ref, o_ref, *, lam_init, nq):
    i = pl.program_id(2)
    dl = dl_ref[...]
    lam = (jnp.exp(jnp.sum(dl[0:1] * dl[1:2], axis=-1, keepdims=True))
           - jnp.exp(jnp.sum(dl[2:3] * dl[3:4], axis=-1, keepdims=True)) + lam_init)

    q = q_ref[...] * (D_A ** -0.5)
    lane = lax.broadcasted_iota(jnp.int32, q.shape, 1)
    zero = jnp.zeros_like(q)
    qq = jnp.concatenate([jnp.where(lane < D_A, q, zero), jnp.where(lane >= D_A, q, zero)], axis=0)
    s = lax.dot_general(qq, k_ref[...], _NT, preferred_element_type=F32)

    bias = [jnp.concatenate([e_ref[m, nq - 1 - i + c] for c in range(nq)], axis=1) for m in range(2)]
    s = s + jnp.concatenate(bias, axis=0)
    e = jnp.exp(s - jnp.max(s, axis=-1, keepdims=True))
    rinv = 1.0 / jnp.sum(e, axis=-1, keepdims=True)
    a = e[:TQ] * rinv[:TQ] - e[TQ:] * (lam * rinv[TQ:])
    o = _dot(a.astype(BF16), v_ref[...])
    o_ref[...] = (_rms(o, sg_ref[...]) * (1.0 - lam_init)).astype(o_ref.dtype)


def _attn_a(qkv, diff_lambda, subln, e_a, l, lam_init):
    b, s, _ = qkv.shape
    nq = s // TQ
    w = 2 * D_A
    return pl.pallas_call(
        functools.partial(_attn_a_kernel, lam_init=lam_init, nq=nq),
        out_shape=jax.ShapeDtypeStruct((b, s, W_A), BF16),
        grid=(H_A, b, nq),
        in_specs=[
            pl.BlockSpec((None, 4, D_A), lambda h, bb, i: (l, 0, 0)),
            pl.BlockSpec((None, TQ, w), lambda h, bb, i: (bb, i, h)),
            pl.BlockSpec((None, s, w), lambda h, bb, i: (bb, 0, W_A // w + h)),
            pl.BlockSpec((None, s, w), lambda h, bb, i: (bb, 0, 2 * W_A // w + h)),
            pl.BlockSpec((None, 2, 2 * nq - 1, TQ, TQ), lambda h, bb, i: (h, 0, 0, 0, 0)),
            pl.BlockSpec((None, 1, w), lambda h, bb, i: (l, 0, 0)),
        ],
        out_specs=pl.BlockSpec((None, TQ, w), lambda h, bb, i: (bb, i, h)),
        compiler_params=_cparams(("parallel", "parallel", "parallel")),
        name="attn_a",
    )(diff_lambda, qkv, qkv, qkv, e_a, subln)


def _b_unit(q, kw, vw, bias, col_ok, acc_scr, m_scr, l_scr, rows, first):
    s = lax.dot_general(q, kw, _NT, preferred_element_type=F32) * (D_B ** -0.5) + bias
    if col_ok is not None:
        s = jnp.where(col_ok, s, NEG_INF)
    m_col = jnp.max(s, axis=-1, keepdims=True)
    if not first:
        m_old = m_scr[rows, :]
        m_col = jnp.maximum(m_old[:, :1], m_col)
        alpha = jnp.exp(m_old - m_col)
    e = jnp.exp(s - m_col)
    l_new = jnp.sum(e, axis=-1, keepdims=True)
    acc = _dot(e.astype(BF16), vw)
    if first:
        l_new = jnp.broadcast_to(l_new, (TL, LANES))
    else:
        l_new = l_scr[rows, :] * alpha + l_new
        acc = acc_scr[rows, :] * alpha + acc
    acc_scr[rows, :] = acc
    l_scr[rows, :] = l_new
    m_scr[rows, :] = jnp.broadcast_to(m_col, (TL, LANES))


def _attn_b_kernel(*refs, seq):
    nat = refs[0:3]
    strided = refs[3:3 + 3 * (N_GROUPS - 1)]
    t_ref, o_ref = refs[-7], refs[-6]
    cls_scrs = refs[-5:-3]
    acc_scr, m_scr, l_scr = refs[-3:]

    kk = lax.broadcasted_iota(jnp.int32, (TL, 2 * TL), 1)
    for g, (_, dil) in enumerate(DILATED_GROUPS):
        cls_len = seq // dil
        halo = _halo(cls_len)
        if dil == 1:
            srcs = [(lambda ref: (lambda sl: ref[sl, :]))(ref) for ref in nat]
        else:
            scr = cls_scrs[g - 1]
            for t in range(3):
                for r in range(dil):
                    scr[t, r] = strided[3 * (g - 1) + t][pl.ds(r, cls_len, stride=dil), :].astype(BF16)
        for r in range(dil):
            if dil != 1:
                srcs = [(lambda t: (lambda sl: scr[t, r, sl, :]))(t) for t in range(3)]
            q_at, k_at, v_at = srcs
            for m0 in range(0, cls_len, TL):
                mid = pl.ds(m0, TL)
                if halo:
                    left = pl.ds((m0 - halo) % cls_len, halo)
                    right = pl.ds((m0 + TL) % cls_len, halo)
                    kw = jnp.concatenate([k_at(left), k_at(mid), k_at(right)], axis=0)
                    vw = jnp.concatenate([v_at(left), v_at(mid), v_at(right)], axis=0)
                    bias = t_ref[g]
                    col_ok = None
                    if m0 == 0:
                        col_ok = kk >= halo
                    if m0 + TL == cls_len:
                        col_ok = kk < halo + TL if col_ok is None else col_ok & (kk < halo + TL)
                else:
                    kw, vw, bias, col_ok = k_at(mid), v_at(mid), t_ref[g, :, :TL], None
                rows = pl.ds(r + dil * m0, TL, stride=dil) if dil != 1 else mid
                _b_unit(q_at(mid), kw, vw, bias, col_ok, acc_scr, m_scr, l_scr, rows, first=(g == 0))
    o_ref[...] = (acc_scr[...] / l_scr[...]).astype(o_ref.dtype)


def _attn_b(qkv_nat, qkv_str, t_b):
    b, s, _ = qkv_nat.shape
    assert DILATED_GROUPS[0][1] == 1
    blk = lambda tile: pl.BlockSpec((None, s, D_B), lambda bb, hh: (bb, 0, tile * HB + hh))
    in_specs = [blk(NAT_QB0), blk(NAT_KB0), blk(NAT_VB0)]
    args = [qkv_nat] * 3
    for g in range(1, N_GROUPS):
        for t in range(3):
            in_specs.append(blk(STR_TILES.index(N_TILES_A + 3 * t + g)))
            args.append(qkv_str)
    in_specs.append(pl.BlockSpec((N_GROUPS, None, TL, 2 * TL), lambda bb, hh: (0, hh, 0, 0)))
    args.append(t_b)
    cls_scrs = [pltpu.VMEM((3, dil, s // dil, D_B), BF16) for _, dil in DILATED_GROUPS[1:]]
    return pl.pallas_call(
        functools.partial(_attn_b_kernel, seq=s),
        out_shape=jax.ShapeDtypeStruct((b, s, W_B), BF16),
        grid=(b, HB),
        in_specs=in_specs,
        out_specs=pl.BlockSpec((None, s, D_B), lambda bb, hh: (bb, 0, hh)),
        scratch_shapes=cls_scrs + [pltpu.VMEM((s, LANES), F32)] * 3,
        compiler_params=_cparams(("parallel", "parallel")),
        name="attn_b",
    )(*args)


def _merge_kernel(x_ref, g_ref, oa_ref, ob_ref, qc_ref, kv_ref, wg0_ref, wg1_ref, wg2_ref, bg_ref,
                  wpa_ref, wpb_ref, wpc_ref, o_ref, h_scr, oc_scr):
    @pl.when(pl.program_id(1) == 0)
    def _():
        h_scr[...] = _rms(x_ref[...], g_ref[...]).astype(BF16)
        for hh in range(H_C):
            cs = slice(hh * D_C, (hh + 1) * D_C)
            vs = slice(W_C + hh * D_C, W_C + (hh + 1) * D_C)
            s = lax.dot_general(qc_ref[:, cs], kv_ref[:, cs], _NT, preferred_element_type=F32) * (D_C ** -0.5)
            e = jnp.exp(s - jnp.max(s, axis=-1, keepdims=True))
            p = e / jnp.sum(e, axis=-1, keepdims=True)
            oc_scr[:, cs] = _dot(p.astype(BF16), kv_ref[:, vs]).astype(BF16)

    h = h_scr[...]

    def gate(w_ref, br):
        return jax.nn.sigmoid(_dot(h, w_ref[...]) + bg_ref[br])

    merged = (gate(wg0_ref, 0) * _dot(oa_ref[...], wpa_ref[...])
              + gate(wg1_ref, 1) * _dot(ob_ref[...], wpb_ref[...])
              + gate(wg2_ref, 2) * _dot(oc_scr[...], wpc_ref[...]))
    o_ref[...] = merged.astype(o_ref.dtype)


def _merge(x, gains, o_a, o_b, qkv, mem_kv, w_gate, b_gate, w_pa, w_pb, w_pc, l, seq, *, tm, tn):
    m, d = x.shape
    nd = d // tn
    per_batch = seq // tm

    def wg_spec(br):
        return pl.BlockSpec((None, d, tn), lambda i, j: (l, 0, br * nd + j))

    return pl.pallas_call(
        _merge_kernel,
        out_shape=jax.ShapeDtypeStruct((m, d), BF16),
        grid=(m // tm, nd),
        in_specs=[
            pl.BlockSpec((tm, d), lambda i, j: (i, 0)),
            pl.BlockSpec((None, 1, d), lambda i, j: (l, 0, 0)),
            pl.BlockSpec((tm, W_A), lambda i, j: (i, 0)),
            pl.BlockSpec((tm, W_B), lambda i, j: (i, 0)),
            pl.BlockSpec((tm, W_C), lambda i, j: (i, NAT_QC)),
            pl.BlockSpec((MEM_LEN, 2 * W_C), lambda i, j: (i // per_batch, 0)),
            wg_spec(0), wg_spec(1), wg_spec(2),
            pl.BlockSpec((None, 3, 1, tn), lambda i, j: (l, 0, 0, j)),
            pl.BlockSpec((None, W_A, tn), lambda i, j: (l, 0, j)),
            pl.BlockSpec((None, W_B, tn), lambda i, j: (l, 0, j)),
            pl.BlockSpec((None, W_C, tn), lambda i, j: (l, 0, j)),
        ],
        out_specs=pl.BlockSpec((tm, tn), lambda i, j: (i, j)),
        scratch_shapes=[pltpu.VMEM((tm, d), BF16), pltpu.VMEM((tm, W_C), BF16)],
        compiler_params=_cparams(("parallel", "arbitrary")),
        name="merge",
    )(x, gains, o_a, o_b, qkv, mem_kv, w_gate, w_gate, w_gate, b_gate, w_pa, w_pb, w_pc)


def _out_proj_kernel(x_ref, m_ref, w_ref, o_ref):
    o_ref[...] = x_ref[...] + _dot(m_ref[...], w_ref[...])


def _out_proj(x, merged, w_out, l, *, tm, tn):
    m, d = x.shape
    return pl.pallas_call(
        _out_proj_kernel,
        out_shape=jax.ShapeDtypeStruct((m, d), F32),
        grid=(m // tm, d // tn),
        in_specs=[
            pl.BlockSpec((tm, tn), lambda i, j: (i, j)),
            pl.BlockSpec((tm, d), lambda i, j: (i, 0)),
            pl.BlockSpec((None, d, tn), lambda i, j: (l, 0, j)),
        ],
        out_specs=pl.BlockSpec((tm, tn), lambda i, j: (i, j)),
        compiler_params=_cparams(("parallel", "parallel")),
        name="out_proj",
    )(x, merged, w_out)


def _ffn_kernel(x_ref, g_ref, wg_ref, wu_ref, wd_ref, fg_ref, o_ref, h_scr, acc_scr, *, final):
    j = pl.program_id(1)

    @pl.when(j == 0)
    def _():
        h_scr[...] = _rms(x_ref[...], g_ref[...]).astype(BF16)
        acc_scr[...] = jnp.zeros_like(acc_scr)

    h = h_scr[...]
    gt = _dot(h, wg_ref[...])
    act = gt * jax.nn.sigmoid(gt) * _dot(h, wu_ref[...])
    acc_scr[...] += _dot(act.astype(BF16), wd_ref[...])

    @pl.when(j == pl.num_programs(1) - 1)
    def _():
        y = x_ref[...] + acc_scr[...]
        if final:
            y = _rms(y, fg_ref[...])
        o_ref[...] = y


def _ffn(x, gains, w_g, w_u, w_d, final_gain, l, final, *, tm, tf):
    m, d = x.shape
    f = w_g.shape[-1]
    return pl.pallas_call(
        functools.partial(_ffn_kernel, final=final),
        out_shape=jax.ShapeDtypeStruct((m, d), F32),
        grid=(m // tm, f // tf),
        in_specs=[
            pl.BlockSpec((tm, d), lambda i, j: (i, 0)),
            pl.BlockSpec((None, 1, d), lambda i, j: (l, 0, 0)),
            pl.BlockSpec((None, d, tf), lambda i, j: (l, 0, j)),
            pl.BlockSpec((None, d, tf), lambda i, j: (l, 0, j)),
            pl.BlockSpec((None, tf, d), lambda i, j: (l, j, 0)),
            pl.BlockSpec((1, d), lambda i, j: (0, 0)),
        ],
        out_specs=pl.BlockSpec((tm, d), lambda i, j: (i, 0)),
        scratch_shapes=[pltpu.VMEM((tm, d), BF16), pltpu.VMEM((tm, d), F32)],
        compiler_params=_cparams(("parallel", "arbitrary")),
        name="ffn",
    )(x, gains, w_g, w_u, w_d, final_gain)


def kernel(x, mem, rel_bias, mem_norm, attn_norm, w_in, diff_lambda, diff_subln, w_mem_kv, w_gate, b_gate,
           w_proj_a, w_proj_b, w_proj_c, w_out, ffn_norm, w_ffn_gate, w_ffn_up, w_ffn_down, final_norm):
    b, s, d = x.shape
    depth = w_in.shape[0]
    assert d == D_MODEL and s % TQ == 0
    assert all(win // (2 * dil) == HALF and (s // dil) % TL == 0 for win, dil in DILATED_GROUPS)
    xf = x.reshape(b * s, d)
    memf = mem.reshape(b * MEM_LEN, d)

    w_in, w_mem_kv, w_gate, w_proj_a, w_proj_b, w_proj_c, w_out, w_ffn_gate, w_ffn_up, w_ffn_down = (
        t.astype(BF16) for t in (w_in, w_mem_kv, w_gate, w_proj_a, w_proj_b, w_proj_c, w_out,
                                 w_ffn_gate, w_ffn_up, w_ffn_down))
    attn_norm = attn_norm.reshape(depth, 1, d)
    ffn_norm = ffn_norm.reshape(depth, 1, d)
    mem_norm = mem_norm.reshape(1, 1, d)
    final_norm = final_norm.reshape(1, d)
    subln = diff_subln.reshape(depth, 1, 2 * D_A)
    b_gate = b_gate.reshape(depth, 3, 1, d)

    e_a = _bias_a(rel_bias, s // TQ)
    t_b = _bias_b(rel_bias, s)

    for l in range(depth):
        lam_init = 0.8 - 0.6 * math.exp(-0.3 * l)
        qkv_nat = _rms_matmul(xf, attn_norm, w_in, l, l, tm=1024, tn=W_B, n_out=len(NAT_TILES) * W_B,
                              col_tile=_tile_lookup(NAT_TILES), out_dtype=BF16)
        qkv_str = _rms_matmul(xf, attn_norm, w_in, l, l, tm=1024, tn=W_B, n_out=len(STR_TILES) * W_B,
                              col_tile=_tile_lookup(STR_TILES), out_dtype=F32)
        mem_kv = _rms_matmul(memf, mem_norm, w_mem_kv, l, 0, tm=512, tn=2 * W_C, n_out=2 * W_C,
                             col_tile=lambda j: j, out_dtype=BF16)
        qkv_nat3 = qkv_nat.reshape(b, s, -1)
        o_a = _attn_a(qkv_nat3, diff_lambda, subln, e_a, l, lam_init).reshape(b * s, W_A)
        o_b = _attn_b(qkv_nat3, qkv_str.reshape(b, s, -1), t_b).reshape(b * s, W_B)
        merged = _merge(xf, attn_norm, o_a, o_b, qkv_nat, mem_kv, w_gate, b_gate,
                        w_proj_a, w_proj_b, w_proj_c, l, s, tm=512, tn=512)
        x1 = _out_proj(xf, merged, w_out, l, tm=1024, tn=1024)
        xf = _ffn(x1, ffn_norm, w_ffn_gate, w_ffn_up, w_ffn_down, final_norm, l, l == depth - 1,
                  tm=512, tf=512)
    return xf.reshape(b, s, d)
```

```python
import functools
import math

import jax
import jax.numpy as jnp
from jax import lax
from jax.experimental import pallas as pl
from jax.experimental.pallas import tpu as pltpu

F32 = jnp.float32
BF16 = jnp.bfloat16

D_MODEL = 2048
MEM_LEN = 256
EPS = 1e-5
NEG_INF = -1e30
H_A = 8
D_A = 64
W_A = H_A * 2 * D_A
DILATED_GROUPS = ((128, 1), (512, 4), (2048, 16))
N_GROUPS = 3
HB = 4
D_B = 128
W_BQKV = N_GROUPS * HB * D_B
W_B = HB * D_B
H_C = 4
D_C = 128
W_C = H_C * D_C
N_IN = 3 * W_A + 3 * W_BQKV + W_C
NUM_BUCKETS = 32
REL_MAX_DISTANCE = 1024

LOG2E = math.log2(math.e)
TQ = 128
A_CHUNK = 512
TL = 128
HALF = 64
LANES = 128
VMEM_LIMIT = 56 * 1024 * 1024

N_TILES_A = 3 * W_A // W_B
NAT_TILES = tuple(range(N_TILES_A)) + tuple(N_TILES_A + 3 * t for t in range(4))
STR_TILES = tuple(N_TILES_A + 3 * t + g for t in range(3) for g in (1, 2))
NAT_QB0, NAT_KB0, NAT_VB0, NAT_QC = (N_TILES_A + t for t in range(4))

_NT = (((1,), (1,)), ((), ()))


def _rms(xf, g):
    return xf * lax.rsqrt(jnp.mean(xf * xf, axis=-1, keepdims=True) + EPS) * g


def _dot(a, b):
    return jnp.dot(a, b, preferred_element_type=F32)


def _cparams(sem):
    return pltpu.CompilerParams(dimension_semantics=sem, vmem_limit_bytes=VMEM_LIMIT)


def _rel_bucket(rel):
    half_b = NUM_BUCKETS // 2
    max_exact = half_b // 2
    n = jnp.abs(rel)
    nf = jnp.maximum(n, 1).astype(F32)
    large = max_exact + (jnp.log(nf / max_exact) / math.log(REL_MAX_DISTANCE / max_exact)
                         * (half_b - max_exact)).astype(jnp.int32)
    large = jnp.minimum(large, half_b - 1)
    return jnp.where(rel > 0, half_b, 0) + jnp.where(n < max_exact, n, large)


def _lookup(bucket, tab_ref, col):
    out = jnp.zeros(bucket.shape, F32)
    for b in range(NUM_BUCKETS):
        out = jnp.where(bucket == b, tab_ref[b, col], out)
    return out


def _bias_a_kernel(tab_ref, e_ref, *, nq):
    j = pl.program_id(0)
    kk = lax.broadcasted_iota(jnp.int32, (TQ, TQ), 0)
    r = lax.broadcasted_iota(jnp.int32, (TQ, TQ), 1)
    bucket = _rel_bucket((j - (nq - 1)) * TQ + kk - r)
    for h in range(H_A):
        for m in range(2):
            e_ref[h, :, m * TQ:(m + 1) * TQ] = _lookup(bucket, tab_ref, m * H_A + h) * LOG2E


def _bias_a(rel_bias, nq):
    nt = 2 * nq - 1
    return pl.pallas_call(
        functools.partial(_bias_a_kernel, nq=nq),
        out_shape=jax.ShapeDtypeStruct((H_A, nt, TQ, 2 * TQ), F32),
        grid=(nt,),
        in_specs=[pl.BlockSpec(memory_space=pltpu.SMEM)],
        out_specs=pl.BlockSpec((H_A, None, TQ, 2 * TQ), lambda j: (0, j, 0, 0)),
        compiler_params=_cparams(("parallel",)),
        name="bias_a",
    )(rel_bias)


def _halo(cls_len):
    return 0 if cls_len == TL else HALF


def _bias_b_kernel(tab_ref, t_ref, *, seq):
    g = pl.program_id(0)
    r = lax.broadcasted_iota(jnp.int32, (TL, 2 * TL), 0)
    kk = lax.broadcasted_iota(jnp.int32, (TL, 2 * TL), 1)
    halo, dil = _halo(seq // DILATED_GROUPS[0][1]), DILATED_GROUPS[0][1]
    for gg in range(1, N_GROUPS):
        halo = jnp.where(g == gg, _halo(seq // DILATED_GROUPS[gg][1]), halo)
        dil = jnp.where(g == gg, DILATED_GROUPS[gg][1], dil)
    step = kk - halo - r
    bucket = _rel_bucket(step * dil)
    for hh in range(HB):
        t_ref[hh] = jnp.where(jnp.abs(step) <= HALF, _lookup(bucket, tab_ref, 2 * H_A + g * HB + hh), NEG_INF)


def _bias_b(rel_bias, seq):
    return pl.pallas_call(
        functools.partial(_bias_b_kernel, seq=seq),
        out_shape=jax.ShapeDtypeStruct((N_GROUPS, HB, TL, 2 * TL), F32),
        grid=(N_GROUPS,),
        in_specs=[pl.BlockSpec(memory_space=pltpu.SMEM)],
        out_specs=pl.BlockSpec((None, HB, TL, 2 * TL), lambda g: (g, 0, 0, 0)),
        compiler_params=_cparams(("parallel",)),
        name="bias_b",
    )(rel_bias)


def _rms_matmul_kernel(x_ref, g_ref, w_ref, o_ref, h_scr, *, lead_tiles, lead_scale):
    j = pl.program_id(1)

    @pl.when(j == 0)
    def _():
        h_scr[...] = _rms(x_ref[...], g_ref[...]).astype(BF16)

    acc = _dot(h_scr[...], w_ref[...])
    if lead_tiles:
        acc = acc * jnp.where(j < lead_tiles, lead_scale, 1.0)
    o_ref[...] = acc.astype(o_ref.dtype)


def _tile_lookup(tiles):
    def f(j):
        out = tiles[0]
        for idx, t in enumerate(tiles[1:], 1):
            out = jnp.where(j == idx, t, out)
        return out
    return f


def _rms_matmul(x, gains, w, l, lg, *, tm, tn, n_out, col_tile, out_dtype, lead_tiles=0, lead_scale=1.0):
    m, k = x.shape
    return pl.pallas_call(
        functools.partial(_rms_matmul_kernel, lead_tiles=lead_tiles, lead_scale=lead_scale),
        out_shape=jax.ShapeDtypeStruct((m, n_out), out_dtype),
        grid=(m // tm, n_out // tn),
        in_specs=[
            pl.BlockSpec((tm, k), lambda i, j: (i, 0)),
            pl.BlockSpec((None, 1, k), lambda i, j: (lg, 0, 0)),
            pl.BlockSpec((None, k, tn), lambda i, j: (l, 0, col_tile(j))),
        ],
        out_specs=pl.BlockSpec((tm, tn), lambda i, j: (i, j)),
        scratch_shapes=[pltpu.VMEM((tm, k), BF16)],
        compiler_params=_cparams(("parallel", "arbitrary")),
        name="rms_matmul",
    )(x, gains, w)


def _attn_a_kernel(dl_ref, q_ref, k_ref, v_ref, e_ref, sg_ref, o_ref, s_scr, p_scr, *, lam_init, nq):
    seq = k_ref.shape[0]
    nch = seq // A_CHUNK
    tiles_per_chunk = A_CHUNK // TQ
    dl = dl_ref[...]
    lam = (jnp.exp(jnp.sum(dl[0:1] * dl[1:2], axis=-1, keepdims=True))
           - jnp.exp(jnp.sum(dl[2:3] * dl[3:4], axis=-1, keepdims=True)) + lam_init)
    lane = lax.broadcasted_iota(jnp.int32, (TQ, 2 * D_A), 1)
    zero = jnp.zeros((TQ, 2 * D_A), BF16)
    qq, col_max, col_sum = {}, {}, {}

    def part_reduce(x, op):
        return op(x.reshape(A_CHUNK // 8, 8, 2 * TQ), axis=0)

    def stage_a(t, c):
        if c == 0:
            q = q_ref[t * TQ:(t + 1) * TQ, :]
            qq[t] = jnp.concatenate([jnp.where(lane < D_A, q, zero), jnp.where(lane >= D_A, q, zero)], axis=0)
        keys = slice(c * A_CHUNK, (c + 1) * A_CHUNK)
        s = lax.dot_general(k_ref[keys, :], qq[t], _NT, preferred_element_type=F32)
        j0 = c * tiles_per_chunk - t + nq - 1
        s = s + e_ref[j0:j0 + tiles_per_chunk].reshape(A_CHUNK, 2 * TQ)
        s_scr[t % 2, keys, :] = s
        part = part_reduce(s, jnp.max)
        col_max[t] = part if c == 0 else jnp.maximum(col_max[t], part)
        if c == nch - 1:
            col_max[t] = jnp.max(col_max[t], axis=0, keepdims=True)
            del qq[t]

    def stage_b(t, c):
        keys = slice(c * A_CHUNK, (c + 1) * A_CHUNK)
        p = jnp.exp2(s_scr[t % 2, keys, :] - col_max[t])
        p_scr[t % 2, keys, :] = p.astype(BF16)
        part = part_reduce(p, jnp.sum)
        col_sum[t] = part if c == 0 else col_sum[t] + part
        if c == nch - 1:
            col_sum[t] = jnp.sum(col_sum[t], axis=0, keepdims=True)
            del col_max[t]

    def stage_c(t):
        ot = lax.dot_general(v_ref[...], p_scr[t % 2], (((0,), (0,)), ((), ())), preferred_element_type=F32)
        rinv = 1.0 / col_sum.pop(t)
        ot = ot[:, :TQ] * rinv[:, :TQ] - ot[:, TQ:] * (lam * rinv[:, TQ:])
        o = ot.T
        o_ref[t * TQ:(t + 1) * TQ, :] = (_rms(o, sg_ref[...]) * (1.0 - lam_init)).astype(o_ref.dtype)

    for step in range(nq + 2):
        for c in range(nch):
            if step < nq:
                stage_a(step, c)
            if 0 <= step - 1 < nq:
                stage_b(step - 1, c)
            if c == 0 and 0 <= step - 2 < nq:
                stage_c(step - 2)


def _attn_a(qkv, diff_lambda, subln, e_a, l, lam_init):
    b, s, _ = qkv.shape
    nq = s // TQ
    assert s % A_CHUNK == 0 and A_CHUNK % TQ == 0
    w = 2 * D_A
    return pl.pallas_call(
        functools.partial(_attn_a_kernel, lam_init=lam_init, nq=nq),
        out_shape=jax.ShapeDtypeStruct((b, s, W_A), BF16),
        grid=(H_A, b),
        in_specs=[
            pl.BlockSpec((None, 4, D_A), lambda h, bb: (l, 0, 0)),
            pl.BlockSpec((None, s, w), lambda h, bb: (bb, 0, h)),
            pl.BlockSpec((None, s, w), lambda h, bb: (bb, 0, W_A // w + h)),
            pl.BlockSpec((None, s, w), lambda h, bb: (bb, 0, 2 * W_A // w + h)),
            pl.BlockSpec((None, 2 * nq - 1, TQ, 2 * TQ), lambda h, bb: (h, 0, 0, 0)),
            pl.BlockSpec((None, 1, w), lambda h, bb: (l, 0, 0)),
        ],
        out_specs=pl.BlockSpec((None, s, w), lambda h, bb: (bb, 0, h)),
        scratch_shapes=[pltpu.VMEM((2, s, 2 * TQ), F32), pltpu.VMEM((2, s, 2 * TQ), BF16)],
        compiler_params=_cparams(("parallel", "parallel")),
        name="attn_a",
    )(diff_lambda, qkv, qkv, qkv, e_a, subln)


def _b_unit(q, kw, vw, bias, col_ok, acc_scr, m_scr, l_scr, rows, first):
    s = lax.dot_general(q, kw, _NT, preferred_element_type=F32) * (D_B ** -0.5) + bias
    if col_ok is not None:
        s = jnp.where(col_ok, s, NEG_INF)
    m_col = jnp.max(s, axis=-1, keepdims=True)
    if not first:
        m_old = m_scr[rows, :]
        m_col = jnp.maximum(m_old[:, :1], m_col)
        alpha = jnp.exp(m_old - m_col)
    e = jnp.exp(s - m_col)
    l_new = jnp.sum(e, axis=-1, keepdims=True)
    acc = _dot(e.astype(BF16), vw)
    if first:
        l_new = jnp.broadcast_to(l_new, (TL, LANES))
    else:
        l_new = l_scr[rows, :] * alpha + l_new
        acc = acc_scr[rows, :] * alpha + acc
    acc_scr[rows, :] = acc
    l_scr[rows, :] = l_new
    m_scr[rows, :] = jnp.broadcast_to(m_col, (TL, LANES))


def _attn_b_kernel(*refs, seq):
    nat = refs[0:3]
    strided = refs[3:3 + 3 * (N_GROUPS - 1)]
    t_ref, o_ref = refs[-7], refs[-6]
    cls_scrs = refs[-5:-3]
    acc_scr, m_scr, l_scr = refs[-3:]

    kk = lax.broadcasted_iota(jnp.int32, (TL, 2 * TL), 1)
    for g, (_, dil) in enumerate(DILATED_GROUPS):
        cls_len = seq // dil
        halo = _halo(cls_len)
        if dil == 1:
            srcs = [(lambda ref: (lambda sl: ref[sl, :]))(ref) for ref in nat]
        else:
            scr = cls_scrs[g - 1]
            for t in range(3):
                for r in range(dil):
                    scr[t, r] = strided[3 * (g - 1) + t][pl.ds(r, cls_len, stride=dil), :].astype(BF16)
        for r in range(dil):
            if dil != 1:
                srcs = [(lambda t: (lambda sl: scr[t, r, sl, :]))(t) for t in range(3)]
            q_at, k_at, v_at = srcs
            for m0 in range(0, cls_len, TL):
                mid = pl.ds(m0, TL)
                if halo:
                    left = pl.ds((m0 - halo) % cls_len, halo)
                    right = pl.ds((m0 + TL) % cls_len, halo)
                    kw = jnp.concatenate([k_at(left), k_at(mid), k_at(right)], axis=0)
                    vw = jnp.concatenate([v_at(left), v_at(mid), v_at(right)], axis=0)
                    bias = t_ref[g]
                    col_ok = None
                    if m0 == 0:
                        col_ok = kk >= halo
                    if m0 + TL == cls_len:
                        col_ok = kk < halo + TL if col_ok is None else col_ok & (kk < halo + TL)
                else:
                    kw, vw, bias, col_ok = k_at(mid), v_at(mid), t_ref[g, :, :TL], None
                rows = pl.ds(r + dil * m0, TL, stride=dil) if dil != 1 else mid
                _b_unit(q_at(mid), kw, vw, bias, col_ok, acc_scr, m_scr, l_scr, rows, first=(g == 0))
    o_ref[...] = (acc_scr[...] / l_scr[...]).astype(o_ref.dtype)


def _attn_b(qkv_nat, qkv_str, t_b):
    b, s, _ = qkv_nat.shape
    assert DILATED_GROUPS[0][1] == 1
    blk = lambda tile: pl.BlockSpec((None, s, D_B), lambda bb, hh: (bb, 0, tile * HB + hh))
    in_specs = [blk(NAT_QB0), blk(NAT_KB0), blk(NAT_VB0)]
    args = [qkv_nat] * 3
    for g in range(1, N_GROUPS):
        for t in range(3):
            in_specs.append(blk(STR_TILES.index(N_TILES_A + 3 * t + g)))
            args.append(qkv_str)
    in_specs.append(pl.BlockSpec((N_GROUPS, None, TL, 2 * TL), lambda bb, hh: (0, hh, 0, 0)))
    args.append(t_b)
    cls_scrs = [pltpu.VMEM((3, dil, s // dil, D_B), BF16) for _, dil in DILATED_GROUPS[1:]]
    return pl.pallas_call(
        functools.partial(_attn_b_kernel, seq=s),
        out_shape=jax.ShapeDtypeStruct((b, s, W_B), BF16),
        grid=(b, HB),
        in_specs=in_specs,
        out_specs=pl.BlockSpec((None, s, D_B), lambda bb, hh: (bb, 0, hh)),
        scratch_shapes=cls_scrs + [pltpu.VMEM((s, LANES), F32)] * 3,
        compiler_params=_cparams(("parallel", "parallel")),
        name="attn_b",
    )(*args)


def _merge_kernel(x_ref, g_ref, oa_ref, ob_ref, qc_ref, kv_ref, wg0_ref, wg1_ref, wg2_ref, bg_ref,
                  wpa_ref, wpb_ref, wpc_ref, o_ref, h_scr, oc_scr):
    @pl.when(pl.program_id(1) == 0)
    def _():
        h_scr[...] = _rms(x_ref[...], g_ref[...]).astype(BF16)
        for hh in range(H_C):
            cs = slice(hh * D_C, (hh + 1) * D_C)
            vs = slice(W_C + hh * D_C, W_C + (hh + 1) * D_C)
            s = lax.dot_general(qc_ref[:, cs], kv_ref[:, cs], _NT, preferred_element_type=F32) * (D_C ** -0.5)
            e = jnp.exp(s - jnp.max(s, axis=-1, keepdims=True))
            p = e / jnp.sum(e, axis=-1, keepdims=True)
            oc_scr[:, cs] = _dot(p.astype(BF16), kv_ref[:, vs]).astype(BF16)

    h = h_scr[...]

    def gate(w_ref, br):
        return jax.nn.sigmoid(_dot(h, w_ref[...]) + bg_ref[br])

    merged = (gate(wg0_ref, 0) * _dot(oa_ref[...], wpa_ref[...])
              + gate(wg1_ref, 1) * _dot(ob_ref[...], wpb_ref[...])
              + gate(wg2_ref, 2) * _dot(oc_scr[...], wpc_ref[...]))
    o_ref[...] = merged.astype(o_ref.dtype)


def _merge(x, gains, o_a, o_b, qkv, mem_kv, w_gate, b_gate, w_pa, w_pb, w_pc, l, seq, *, tm, tn):
    m, d = x.shape
    nd = d // tn
    per_batch = seq // tm

    def wg_spec(br):
        return pl.BlockSpec((None, d, tn), lambda i, j: (l, 0, br * nd + j))

    return pl.pallas_call(
        _merge_kernel,
        out_shape=jax.ShapeDtypeStruct((m, d), BF16),
        grid=(m // tm, nd),
        in_specs=[
            pl.BlockSpec((tm, d), lambda i, j: (i, 0)),
            pl.BlockSpec((None, 1, d), lambda i, j: (l, 0, 0)),
            pl.BlockSpec((tm, W_A), lambda i, j: (i, 0)),
            pl.BlockSpec((tm, W_B), lambda i, j: (i, 0)),
            pl.BlockSpec((tm, W_C), lambda i, j: (i, NAT_QC)),
            pl.BlockSpec((MEM_LEN, 2 * W_C), lambda i, j: (i // per_batch, 0)),
            wg_spec(0), wg_spec(1), wg_spec(2),
            pl.BlockSpec((None, 3, 1, tn), lambda i, j: (l, 0, 0, j)),
            pl.BlockSpec((None, W_A, tn), lambda i, j: (l, 0, j)),
            pl.BlockSpec((None, W_B, tn), lambda i, j: (l, 0, j)),
            pl.BlockSpec((None, W_C, tn), lambda i, j: (l, 0, j)),
        ],
        out_specs=pl.BlockSpec((tm, tn), lambda i, j: (i, j)),
        scratch_shapes=[pltpu.VMEM((tm, d), BF16), pltpu.VMEM((tm, W_C), BF16)],
        compiler_params=_cparams(("parallel", "arbitrary")),
        name="merge",
    )(x, gains, o_a, o_b, qkv, mem_kv, w_gate, w_gate, w_gate, b_gate, w_pa, w_pb, w_pc)


def _out_proj_kernel(x_ref, m_ref, w_ref, o_ref):
    o_ref[...] = x_ref[...] + _dot(m_ref[...], w_ref[...])


def _out_proj(x, merged, w_out, l, *, tm, tn):
    m, d = x.shape
    return pl.pallas_call(
        _out_proj_kernel,
        out_shape=jax.ShapeDtypeStruct((m, d), F32),
        grid=(m // tm, d // tn),
        in_specs=[
            pl.BlockSpec((tm, tn), lambda i, j: (i, j)),
            pl.BlockSpec((tm, d), lambda i, j: (i, 0)),
            pl.BlockSpec((None, d, tn), lambda i, j: (l, 0, j)),
        ],
        out_specs=pl.BlockSpec((tm, tn), lambda i, j: (i, j)),
        compiler_params=_cparams(("parallel", "parallel")),
        name="out_proj",
    )(x, merged, w_out)


def _ffn_kernel(x_ref, g_ref, wg_ref, wu_ref, wd_ref, fg_ref, o_ref, h_scr, acc_scr, *, final):
    j = pl.program_id(1)

    @pl.when(j == 0)
    def _():
        h_scr[...] = _rms(x_ref[...], g_ref[...]).astype(BF16)
        acc_scr[...] = jnp.zeros_like(acc_scr)

    h = h_scr[...]
    gt = _dot(h, wg_ref[...])
    act = gt * jax.nn.sigmoid(gt) * _dot(h, wu_ref[...])
    acc_scr[...] += _dot(act.astype(BF16), wd_ref[...])

    @pl.when(j == pl.num_programs(1) - 1)
    def _():
        y = x_ref[...] + acc_scr[...]
        if final:
            y = _rms(y, fg_ref[...])
        o_ref[...] = y


def _ffn(x, gains, w_g, w_u, w_d, final_gain, l, final, *, tm, tf):
    m, d = x.shape
    f = w_g.shape[-1]
    return pl.pallas_call(
        functools.partial(_ffn_kernel, final=final),
        out_shape=jax.ShapeDtypeStruct((m, d), F32),
        grid=(m // tm, f // tf),
        in_specs=[
            pl.BlockSpec((tm, d), lambda i, j: (i, 0)),
            pl.BlockSpec((None, 1, d), lambda i, j: (l, 0, 0)),
            pl.BlockSpec((None, d, tf), lambda i, j: (l, 0, j)),
            pl.BlockSpec((None, d, tf), lambda i, j: (l, 0, j)),
            pl.BlockSpec((None, tf, d), lambda i, j: (l, j, 0)),
            pl.BlockSpec((1, d), lambda i, j: (0, 0)),
        ],
        out_specs=pl.BlockSpec((tm, d), lambda i, j: (i, 0)),
        scratch_shapes=[pltpu.VMEM((tm, d), BF16), pltpu.VMEM((tm, d), F32)],
        compiler_params=_cparams(("parallel", "arbitrary")),
        name="ffn",
    )(x, gains, w_g, w_u, w_d, final_gain)


def kernel(x, mem, rel_bias, mem_norm, attn_norm, w_in, diff_lambda, diff_subln, w_mem_kv, w_gate, b_gate,
           w_proj_a, w_proj_b, w_proj_c, w_out, ffn_norm, w_ffn_gate, w_ffn_up, w_ffn_down, final_norm):
    b, s, d = x.shape
    depth = w_in.shape[0]
    assert d == D_MODEL and s % TQ == 0
    assert all(win // (2 * dil) == HALF and (s // dil) % TL == 0 for win, dil in DILATED_GROUPS)
    xf = x.reshape(b * s, d)
    memf = mem.reshape(b * MEM_LEN, d)

    w_in, w_mem_kv, w_gate, w_proj_a, w_proj_b, w_proj_c, w_out, w_ffn_gate, w_ffn_up, w_ffn_down = (
        t.astype(BF16) for t in (w_in, w_mem_kv, w_gate, w_proj_a, w_proj_b, w_proj_c, w_out,
                                 w_ffn_gate, w_ffn_up, w_ffn_down))
    attn_norm = attn_norm.reshape(depth, 1, d)
    ffn_norm = ffn_norm.reshape(depth, 1, d)
    mem_norm = mem_norm.reshape(1, 1, d)
    final_norm = final_norm.reshape(1, d)
    subln = diff_subln.reshape(depth, 1, 2 * D_A)
    b_gate = b_gate.reshape(depth, 3, 1, d)

    e_a = _bias_a(rel_bias, s // TQ)
    t_b = _bias_b(rel_bias, s)

    for l in range(depth):
        lam_init = 0.8 - 0.6 * math.exp(-0.3 * l)
        qkv_nat = _rms_matmul(xf, attn_norm, w_in, l, l, tm=1024, tn=W_B, n_out=len(NAT_TILES) * W_B,
                              col_tile=_tile_lookup(NAT_TILES), out_dtype=BF16,
                              lead_tiles=W_A // W_B, lead_scale=D_A ** -0.5 * LOG2E)
        qkv_str = _rms_matmul(xf, attn_norm, w_in, l, l, tm=1024, tn=W_B, n_out=len(STR_TILES) * W_B,
                              col_tile=_tile_lookup(STR_TILES), out_dtype=F32)
        mem_kv = _rms_matmul(memf, mem_norm, w_mem_kv, l, 0, tm=512, tn=2 * W_C, n_out=2 * W_C,
                             col_tile=lambda j: j, out_dtype=BF16)
        qkv_nat3 = qkv_nat.reshape(b, s, -1)
        o_a = _attn_a(qkv_nat3, diff_lambda, subln, e_a, l, lam_init).reshape(b * s, W_A)
        o_b = _attn_b(qkv_nat3, qkv_str.reshape(b, s, -1), t_b).reshape(b * s, W_B)
        merged = _merge(xf, attn_norm, o_a, o_b, qkv_nat, mem_kv, w_gate, b_gate,
                        w_proj_a, w_proj_b, w_proj_c, l, s, tm=512, tn=512)
        x1 = _out_proj(xf, merged, w_out, l, tm=1024, tn=1024)
        xf = _ffn(x1, ffn_norm, w_ffn_gate, w_ffn_up, w_ffn_down, final_norm, l, l == depth - 1,
                  tm=512, tf=512)
    return xf.reshape(b, s, d)
```

```python
import functools
import math

import jax
import jax.numpy as jnp
from jax import lax
from jax.experimental import pallas as pl
from jax.experimental.pallas import tpu as pltpu

F32 = jnp.float32
BF16 = jnp.bfloat16

D_MODEL = 2048
MEM_LEN = 256
EPS = 1e-5
NEG_INF = -1e30
H_A = 8
D_A = 64
W_A = H_A * 2 * D_A
DILATED_GROUPS = ((128, 1), (512, 4), (2048, 16))
N_GROUPS = 3
HB = 4
D_B = 128
W_BQKV = N_GROUPS * HB * D_B
W_B = HB * D_B
H_C = 4
D_C = 128
W_C = H_C * D_C
N_IN = 3 * W_A + 3 * W_BQKV + W_C
NUM_BUCKETS = 32
REL_MAX_DISTANCE = 1024

LOG2E = math.log2(math.e)
TQ = 128
A_CHUNK = 512
C_ROWS = 512
TL = 128
HALF = 64
LANES = 128
VMEM_LIMIT = 56 * 1024 * 1024

N_TILES_A = 3 * W_A // W_B
NAT_TILES = tuple(range(N_TILES_A)) + tuple(N_TILES_A + 3 * t for t in range(4))
STR_TILES = tuple(N_TILES_A + 3 * t + g for t in range(3) for g in (1, 2))
NAT_QB0, NAT_KB0, NAT_VB0, NAT_QC = (N_TILES_A + t for t in range(4))

_NT = (((1,), (1,)), ((), ()))


def _rms(xf, g):
    return xf * lax.rsqrt(jnp.mean(xf * xf, axis=-1, keepdims=True) + EPS) * g


def _dot(a, b):
    return jnp.dot(a, b, preferred_element_type=F32)


def _cparams(sem):
    return pltpu.CompilerParams(dimension_semantics=sem, vmem_limit_bytes=VMEM_LIMIT)


def _rel_bucket(rel):
    half_b = NUM_BUCKETS // 2
    max_exact = half_b // 2
    n = jnp.abs(rel)
    nf = jnp.maximum(n, 1).astype(F32)
    large = max_exact + (jnp.log(nf / max_exact) / math.log(REL_MAX_DISTANCE / max_exact)
                         * (half_b - max_exact)).astype(jnp.int32)
    large = jnp.minimum(large, half_b - 1)
    return jnp.where(rel > 0, half_b, 0) + jnp.where(n < max_exact, n, large)


def _lookup(bucket, tab_ref, col):
    out = jnp.zeros(bucket.shape, F32)
    for b in range(NUM_BUCKETS):
        out = jnp.where(bucket == b, tab_ref[b, col], out)
    return out


def _bias_a_kernel(tab_ref, e_ref, *, nq):
    j = pl.program_id(0)
    kk = lax.broadcasted_iota(jnp.int32, (TQ, TQ), 0)
    r = lax.broadcasted_iota(jnp.int32, (TQ, TQ), 1)
    bucket = _rel_bucket((j - (nq - 1)) * TQ + kk - r)
    for h in range(H_A):
        for m in range(2):
            e_ref[h, :, m * TQ:(m + 1) * TQ] = _lookup(bucket, tab_ref, m * H_A + h) * LOG2E


def _bias_a(rel_bias, nq):
    nt = 2 * nq - 1
    return pl.pallas_call(
        functools.partial(_bias_a_kernel, nq=nq),
        out_shape=jax.ShapeDtypeStruct((H_A, nt, TQ, 2 * TQ), F32),
        grid=(nt,),
        in_specs=[pl.BlockSpec(memory_space=pltpu.SMEM)],
        out_specs=pl.BlockSpec((H_A, None, TQ, 2 * TQ), lambda j: (0, j, 0, 0)),
        compiler_params=_cparams(("parallel",)),
        name="bias_a",
    )(rel_bias)


def _halo(cls_len):
    return 0 if cls_len == TL else HALF


def _bias_b_kernel(tab_ref, t_ref, *, seq):
    g = pl.program_id(0)
    r = lax.broadcasted_iota(jnp.int32, (TL, 2 * TL), 0)
    kk = lax.broadcasted_iota(jnp.int32, (TL, 2 * TL), 1)
    halo, dil = _halo(seq // DILATED_GROUPS[0][1]), DILATED_GROUPS[0][1]
    for gg in range(1, N_GROUPS):
        halo = jnp.where(g == gg, _halo(seq // DILATED_GROUPS[gg][1]), halo)
        dil = jnp.where(g == gg, DILATED_GROUPS[gg][1], dil)
    step = kk - halo - r
    bucket = _rel_bucket(step * dil)
    for hh in range(HB):
        t_ref[hh] = jnp.where(jnp.abs(step) <= HALF, _lookup(bucket, tab_ref, 2 * H_A + g * HB + hh), NEG_INF)


def _bias_b(rel_bias, seq):
    return pl.pallas_call(
        functools.partial(_bias_b_kernel, seq=seq),
        out_shape=jax.ShapeDtypeStruct((N_GROUPS, HB, TL, 2 * TL), F32),
        grid=(N_GROUPS,),
        in_specs=[pl.BlockSpec(memory_space=pltpu.SMEM)],
        out_specs=pl.BlockSpec((None, HB, TL, 2 * TL), lambda g: (g, 0, 0, 0)),
        compiler_params=_cparams(("parallel",)),
        name="bias_b",
    )(rel_bias)


def _norm_kernel(x_ref, g_ref, o_ref):
    o_ref[...] = _rms(x_ref[...], g_ref[...]).astype(o_ref.dtype)


def _norm(x, gains, lg, *, tm):
    m, d = x.shape
    return pl.pallas_call(
        _norm_kernel,
        out_shape=jax.ShapeDtypeStruct((m, d), BF16),
        grid=(m // tm,),
        in_specs=[pl.BlockSpec((tm, d), lambda i: (i, 0)), pl.BlockSpec((None, 1, d), lambda i: (lg, 0, 0))],
        out_specs=pl.BlockSpec((tm, d), lambda i: (i, 0)),
        compiler_params=_cparams(("parallel",)),
        name="norm",
    )(x, gains)


def _proj_kernel(h_ref, w_ref, o_ref, w_scr, *, lead_tiles, lead_scale):
    j = pl.program_id(0)

    @pl.when(pl.program_id(1) == 0)
    def _():
        w_scr[...] = w_ref[...].astype(BF16)

    acc = _dot(h_ref[...], w_scr[...])
    if lead_tiles:
        acc = acc * jnp.where(j < lead_tiles, lead_scale, 1.0)
    o_ref[...] = acc.astype(o_ref.dtype)


def _tile_lookup(tiles):
    def f(j):
        out = tiles[0]
        for idx, t in enumerate(tiles[1:], 1):
            out = jnp.where(j == idx, t, out)
        return out
    return f


def _proj(h, w, l, *, tm, tn, n_out, col_tile, out_dtype, lead_tiles=0, lead_scale=1.0):
    m, k = h.shape
    return pl.pallas_call(
        functools.partial(_proj_kernel, lead_tiles=lead_tiles, lead_scale=lead_scale),
        out_shape=jax.ShapeDtypeStruct((m, n_out), out_dtype),
        grid=(n_out // tn, m // tm),
        in_specs=[
            pl.BlockSpec((tm, k), lambda j, i: (i, 0)),
            pl.BlockSpec((None, k, tn), lambda j, i: (l, 0, col_tile(j))),
        ],
        out_specs=pl.BlockSpec((tm, tn), lambda j, i: (i, j)),
        scratch_shapes=[pltpu.VMEM((k, tn), BF16)],
        compiler_params=_cparams(("parallel", "arbitrary")),
        name="proj",
    )(h, w)


def _attn_a_kernel(dl_ref, q_ref, k_ref, v_ref, e_ref, sg_ref, o_ref, s_scr, p_scr, *, lam_init, nq):
    seq = k_ref.shape[0]
    nch = seq // A_CHUNK
    tiles_per_chunk = A_CHUNK // TQ
    dl = dl_ref[...]
    lam = (jnp.exp(jnp.sum(dl[0:1] * dl[1:2], axis=-1, keepdims=True))
           - jnp.exp(jnp.sum(dl[2:3] * dl[3:4], axis=-1, keepdims=True)) + lam_init)
    lane = lax.broadcasted_iota(jnp.int32, (TQ, 2 * D_A), 1)
    zero = jnp.zeros((TQ, 2 * D_A), BF16)
    qq, col_max, col_sum = {}, {}, {}

    def part_reduce(x, op):
        return op(x.reshape(A_CHUNK // 8, 8, 2 * TQ), axis=0)

    def stage_a(t, c):
        if c == 0:
            q = q_ref[t * TQ:(t + 1) * TQ, :]
            qq[t] = jnp.concatenate([jnp.where(lane < D_A, q, zero), jnp.where(lane >= D_A, q, zero)], axis=0)
        keys = slice(c * A_CHUNK, (c + 1) * A_CHUNK)
        s = lax.dot_general(k_ref[keys, :], qq[t], _NT, preferred_element_type=F32)
        j0 = c * tiles_per_chunk - t + nq - 1
        s = s + e_ref[j0:j0 + tiles_per_chunk].reshape(A_CHUNK, 2 * TQ)
        s_scr[t % 2, keys, :] = s
        part = part_reduce(s, jnp.max)
        col_max[t] = part if c == 0 else jnp.maximum(col_max[t], part)
        if c == nch - 1:
            col_max[t] = jnp.max(col_max[t], axis=0, keepdims=True)
            del qq[t]

    def stage_b(t, c):
        keys = slice(c * A_CHUNK, (c + 1) * A_CHUNK)
        p = jnp.exp2(s_scr[t % 2, keys, :] - col_max[t])
        p_scr[t % 2, keys, :] = p.astype(BF16)
        part = part_reduce(p, jnp.sum)
        col_sum[t] = part if c == 0 else col_sum[t] + part
        if c == nch - 1:
            col_sum[t] = jnp.sum(col_sum[t], axis=0, keepdims=True)
            del col_max[t]

    def stage_c(t):
        ot = lax.dot_general(v_ref[...], p_scr[t % 2], (((0,), (0,)), ((), ())), preferred_element_type=F32)
        rinv = 1.0 / col_sum.pop(t)
        ot = ot[:, :TQ] * rinv[:, :TQ] - ot[:, TQ:] * (lam * rinv[:, TQ:])
        o = ot.T
        o_ref[t * TQ:(t + 1) * TQ, :] = (_rms(o, sg_ref[...]) * (1.0 - lam_init)).astype(o_ref.dtype)

    for step in range(nq + 2):
        for c in range(nch):
            if step < nq:
                stage_a(step, c)
            if 0 <= step - 1 < nq:
                stage_b(step - 1, c)
            if c == 0 and 0 <= step - 2 < nq:
                stage_c(step - 2)


def _attn_a(qkv, diff_lambda, subln, e_a, l, lam_init):
    b, s, _ = qkv.shape
    nq = s // TQ
    assert s % A_CHUNK == 0 and A_CHUNK % TQ == 0
    w = 2 * D_A
    return pl.pallas_call(
        functools.partial(_attn_a_kernel, lam_init=lam_init, nq=nq),
        out_shape=jax.ShapeDtypeStruct((b, s, W_A), BF16),
        grid=(H_A, b),
        in_specs=[
            pl.BlockSpec((None, 4, D_A), lambda h, bb: (l, 0, 0)),
            pl.BlockSpec((None, s, w), lambda h, bb: (bb, 0, h)),
            pl.BlockSpec((None, s, w), lambda h, bb: (bb, 0, W_A // w + h)),
            pl.BlockSpec((None, s, w), lambda h, bb: (bb, 0, 2 * W_A // w + h)),
            pl.BlockSpec((None, 2 * nq - 1, TQ, 2 * TQ), lambda h, bb: (h, 0, 0, 0)),
            pl.BlockSpec((None, 1, w), lambda h, bb: (l, 0, 0)),
        ],
        out_specs=pl.BlockSpec((None, s, w), lambda h, bb: (bb, 0, h)),
        scratch_shapes=[pltpu.VMEM((2, s, 2 * TQ), F32), pltpu.VMEM((2, s, 2 * TQ), BF16)],
        compiler_params=_cparams(("parallel", "parallel")),
        name="attn_a",
    )(diff_lambda, qkv, qkv, qkv, e_a, subln)


def _b_unit(q, kw, vw, bias, col_ok, acc_scr, m_scr, l_scr, rows, first):
    s = lax.dot_general(q, kw, _NT, preferred_element_type=F32) * (D_B ** -0.5) + bias
    if col_ok is not None:
        s = jnp.where(col_ok, s, NEG_INF)
    m_col = jnp.max(s, axis=-1, keepdims=True)
    if not first:
        m_old = m_scr[rows, :]
        m_col = jnp.maximum(m_old[:, :1], m_col)
        alpha = jnp.exp(m_old - m_col)
    e = jnp.exp(s - m_col)
    l_new = jnp.sum(e, axis=-1, keepdims=True)
    acc = _dot(e.astype(BF16), vw)
    if first:
        l_new = jnp.broadcast_to(l_new, (TL, LANES))
    else:
        l_new = l_scr[rows, :] * alpha + l_new
        acc = acc_scr[rows, :] * alpha + acc
    acc_scr[rows, :] = acc
    l_scr[rows, :] = l_new
    m_scr[rows, :] = jnp.broadcast_to(m_col, (TL, LANES))


def _attn_c(qc_ref, mk_ref, mv_ref, oc_ref):
    for r0 in range(0, qc_ref.shape[0], C_ROWS):
        rows = slice(r0, r0 + C_ROWS)
        s = lax.dot_general(qc_ref[rows, :], mk_ref[...], _NT, preferred_element_type=F32) * (D_C ** -0.5)
        e = jnp.exp(s - jnp.max(s, axis=-1, keepdims=True))
        p = e / jnp.sum(e, axis=-1, keepdims=True)
        oc_ref[rows, :] = _dot(p.astype(BF16), mv_ref[...]).astype(oc_ref.dtype)


def _attn_bc_kernel(*refs, seq):
    nat = refs[0:3]
    strided = refs[3:3 + 3 * (N_GROUPS - 1)]
    t_ref, qc_ref, mk_ref, mv_ref, o_ref, oc_ref = refs[-11:-5]
    cls_scrs = refs[-5:-3]
    acc_scr, m_scr, l_scr = refs[-3:]

    _attn_c(qc_ref, mk_ref, mv_ref, oc_ref)

    kk = lax.broadcasted_iota(jnp.int32, (TL, 2 * TL), 1)
    for g, (_, dil) in enumerate(DILATED_GROUPS):
        cls_len = seq // dil
        halo = _halo(cls_len)
        if dil == 1:
            srcs = [(lambda ref: (lambda sl: ref[sl, :]))(ref) for ref in nat]
        else:
            scr = cls_scrs[g - 1]
            for t in range(3):
                for r in range(dil):
                    scr[t, r] = strided[3 * (g - 1) + t][pl.ds(r, cls_len, stride=dil), :].astype(BF16)
        for r in range(dil):
            if dil != 1:
                srcs = [(lambda t: (lambda sl: scr[t, r, sl, :]))(t) for t in range(3)]
            q_at, k_at, v_at = srcs
            for m0 in range(0, cls_len, TL):
                mid = pl.ds(m0, TL)
                if halo:
                    left = pl.ds((m0 - halo) % cls_len, halo)
                    right = pl.ds((m0 + TL) % cls_len, halo)
                    kw = jnp.concatenate([k_at(left), k_at(mid), k_at(right)], axis=0)
                    vw = jnp.concatenate([v_at(left), v_at(mid), v_at(right)], axis=0)
                    bias = t_ref[g]
                    col_ok = None
                    if m0 == 0:
                        col_ok = kk >= halo
                    if m0 + TL == cls_len:
                        col_ok = kk < halo + TL if col_ok is None else col_ok & (kk < halo + TL)
                else:
                    kw, vw, bias, col_ok = k_at(mid), v_at(mid), t_ref[g, :, :TL], None
                rows = pl.ds(r + dil * m0, TL, stride=dil) if dil != 1 else mid
                _b_unit(q_at(mid), kw, vw, bias, col_ok, acc_scr, m_scr, l_scr, rows, first=(g == 0))
    o_ref[...] = (acc_scr[...] / l_scr[...]).astype(o_ref.dtype)


def _attn_bc(qkv_nat, qkv_str, mem_kv, t_b):
    b, s, _ = qkv_nat.shape
    assert DILATED_GROUPS[0][1] == 1 and HB == H_C and D_B == D_C and s % C_ROWS == 0
    blk = lambda tile: pl.BlockSpec((None, s, D_B), lambda bb, hh: (bb, 0, tile * HB + hh))
    in_specs = [blk(NAT_QB0), blk(NAT_KB0), blk(NAT_VB0)]
    args = [qkv_nat] * 3
    for g in range(1, N_GROUPS):
        for t in range(3):
            in_specs.append(blk(STR_TILES.index(N_TILES_A + 3 * t + g)))
            args.append(qkv_str)
    in_specs += [
        pl.BlockSpec((N_GROUPS, None, TL, 2 * TL), lambda bb, hh: (0, hh, 0, 0)),
        blk(NAT_QC),
        pl.BlockSpec((None, MEM_LEN, D_C), lambda bb, hh: (bb, 0, hh)),
        pl.BlockSpec((None, MEM_LEN, D_C), lambda bb, hh: (bb, 0, H_C + hh)),
    ]
    args += [t_b, qkv_nat, mem_kv, mem_kv]
    cls_scrs = [pltpu.VMEM((3, dil, s // dil, D_B), BF16) for _, dil in DILATED_GROUPS[1:]]
    out = jax.ShapeDtypeStruct((b, s, W_B), BF16)
    out_spec = pl.BlockSpec((None, s, D_B), lambda bb, hh: (bb, 0, hh))
    return pl.pallas_call(
        functools.partial(_attn_bc_kernel, seq=s),
        out_shape=[out, out],
        grid=(b, HB),
        in_specs=in_specs,
        out_specs=[out_spec, out_spec],
        scratch_shapes=cls_scrs + [pltpu.VMEM((s, LANES), F32)] * 3,
        compiler_params=_cparams(("parallel", "parallel")),
        name="attn_bc",
    )(*args)


def _merge_kernel(h_ref, oa_ref, ob_ref, oc_ref, wg0_ref, wg1_ref, wg2_ref, bg_ref, wpa_ref, wpb_ref, wpc_ref,
                  o_ref, wg_scr, wpa_scr, wpb_scr, wpc_scr):
    @pl.when(pl.program_id(1) == 0)
    def _():
        for br, w_ref in enumerate((wg0_ref, wg1_ref, wg2_ref)):
            wg_scr[br] = w_ref[...].astype(BF16)
        wpa_scr[...] = wpa_ref[...].astype(BF16)
        wpb_scr[...] = wpb_ref[...].astype(BF16)
        wpc_scr[...] = wpc_ref[...].astype(BF16)

    h = h_ref[...]

    def gate(br):
        return jax.nn.sigmoid(_dot(h, wg_scr[br]) + bg_ref[br])

    merged = (gate(0) * _dot(oa_ref[...], wpa_scr[...])
              + gate(1) * _dot(ob_ref[...], wpb_scr[...])
              + gate(2) * _dot(oc_ref[...], wpc_scr[...]))
    o_ref[...] = merged.astype(o_ref.dtype)


def _merge(h, o_a, o_b, o_c, w_gate, b_gate, w_pa, w_pb, w_pc, l, *, tm, tn):
    m, d = h.shape
    nd = d // tn

    def wg_spec(br):
        return pl.BlockSpec((None, d, tn), lambda j, i: (l, 0, br * nd + j))

    def rows(width):
        return pl.BlockSpec((tm, width), lambda j, i: (i, 0))

    def wp_spec(width):
        return pl.BlockSpec((None, width, tn), lambda j, i: (l, 0, j))

    return pl.pallas_call(
        _merge_kernel,
        out_shape=jax.ShapeDtypeStruct((m, d), BF16),
        grid=(nd, m // tm),
        in_specs=[
            rows(d), rows(W_A), rows(W_B), rows(W_C),
            wg_spec(0), wg_spec(1), wg_spec(2),
            pl.BlockSpec((None, 3, 1, tn), lambda j, i: (l, 0, 0, j)),
            wp_spec(W_A), wp_spec(W_B), wp_spec(W_C),
        ],
        out_specs=pl.BlockSpec((tm, tn), lambda j, i: (i, j)),
        scratch_shapes=[pltpu.VMEM((3, d, tn), BF16), pltpu.VMEM((W_A, tn), BF16),
                        pltpu.VMEM((W_B, tn), BF16), pltpu.VMEM((W_C, tn), BF16)],
        compiler_params=_cparams(("parallel", "arbitrary")),
        name="merge",
    )(h, o_a, o_b, o_c, w_gate, w_gate, w_gate, b_gate, w_pa, w_pb, w_pc)


def _out_proj_kernel(x_ref, m_ref, w_ref, g_ref, x1_ref, h2_ref):
    y = x_ref[...] + _dot(m_ref[...], w_ref[...])
    x1_ref[...] = y
    h2_ref[...] = _rms(y, g_ref[...]).astype(h2_ref.dtype)


def _out_proj(x, merged, w_out, gains, l, *, tm):
    m, d = x.shape
    row = pl.BlockSpec((tm, d), lambda i: (i, 0))
    return pl.pallas_call(
        _out_proj_kernel,
        out_shape=[jax.ShapeDtypeStruct((m, d), F32), jax.ShapeDtypeStruct((m, d), BF16)],
        grid=(m // tm,),
        in_specs=[row, row, pl.BlockSpec((None, d, d), lambda i: (l, 0, 0)),
                  pl.BlockSpec((None, 1, d), lambda i: (l, 0, 0))],
        out_specs=[row, row],
        compiler_params=_cparams(("parallel",)),
        name="out_proj",
    )(x, merged, w_out, gains)


def _ffn_act_kernel(h_ref, wg_ref, wu_ref, o_ref, wg_scr, wu_scr):
    @pl.when(pl.program_id(1) == 0)
    def _():
        wg_scr[...] = wg_ref[...].astype(BF16)
        wu_scr[...] = wu_ref[...].astype(BF16)

    h = h_ref[...]
    gt = _dot(h, wg_scr[...])
    o_ref[...] = (gt * jax.nn.sigmoid(gt) * _dot(h, wu_scr[...])).astype(o_ref.dtype)


def _ffn_act(h, w_g, w_u, l, *, tm, tf):
    m, d = h.shape
    f = w_g.shape[-1]
    w_spec = pl.BlockSpec((None, d, tf), lambda j, i: (l, 0, j))
    return pl.pallas_call(
        _ffn_act_kernel,
        out_shape=jax.ShapeDtypeStruct((m, f), BF16),
        grid=(f // tf, m // tm),
        in_specs=[pl.BlockSpec((tm, d), lambda j, i: (i, 0)), w_spec, w_spec],
        out_specs=pl.BlockSpec((tm, tf), lambda j, i: (i, j)),
        scratch_shapes=[pltpu.VMEM((d, tf), BF16)] * 2,
        compiler_params=_cparams(("parallel", "arbitrary")),
        name="ffn_act",
    )(h, w_g, w_u)


def _ffn_down_kernel(x_ref, a_ref, w_ref, g_ref, *out_refs, final):
    y = x_ref[...] + _dot(a_ref[...], w_ref[...])
    if final:
        out_refs[0][...] = _rms(y, g_ref[...])
    else:
        out_refs[0][...] = y
        out_refs[1][...] = _rms(y, g_ref[...]).astype(out_refs[1].dtype)


def _ffn_down(x, act, w_d, gains, l, lg, final, *, tm):
    m, d = x.shape
    f = act.shape[-1]
    row = pl.BlockSpec((tm, d), lambda i: (i, 0))
    out_shape = [jax.ShapeDtypeStruct((m, d), F32)]
    if not final:
        out_shape.append(jax.ShapeDtypeStruct((m, d), BF16))
    return pl.pallas_call(
        functools.partial(_ffn_down_kernel, final=final),
        out_shape=out_shape,
        grid=(m // tm,),
        in_specs=[row, pl.BlockSpec((tm, f), lambda i: (i, 0)),
                  pl.BlockSpec((None, f, d), lambda i: (l, 0, 0), pipeline_mode=pl.Buffered(1)),
                  pl.BlockSpec((None, 1, d), lambda i: (lg, 0, 0))],
        out_specs=[row] * len(out_shape),
        compiler_params=_cparams(("parallel",)),
        name="ffn_down",
    )(x, act, w_d, gains)


def kernel(x, mem, rel_bias, mem_norm, attn_norm, w_in, diff_lambda, diff_subln, w_mem_kv, w_gate, b_gate,
           w_proj_a, w_proj_b, w_proj_c, w_out, ffn_norm, w_ffn_gate, w_ffn_up, w_ffn_down, final_norm):
    b, s, d = x.shape
    depth = w_in.shape[0]
    assert d == D_MODEL and s % TQ == 0
    assert all(win // (2 * dil) == HALF and (s // dil) % TL == 0 for win, dil in DILATED_GROUPS)
    xf = x.reshape(b * s, d)
    memf = mem.reshape(b * MEM_LEN, d)

    w_out = w_out.astype(BF16)
    w_ffn_down = w_ffn_down.astype(BF16)
    attn_norm = attn_norm.reshape(depth, 1, d)
    ffn_norm = ffn_norm.reshape(depth, 1, d)
    mem_norm = mem_norm.reshape(1, 1, d)
    final_norm = final_norm.reshape(1, 1, d)
    subln = diff_subln.reshape(depth, 1, 2 * D_A)
    b_gate = b_gate.reshape(depth, 3, 1, d)

    e_a = _bias_a(rel_bias, s // TQ)
    t_b = _bias_b(rel_bias, s)
    mem_n = _norm(memf, mem_norm, 0, tm=512)
    h = _norm(xf, attn_norm, 0, tm=512)

    for l in range(depth):
        lam_init = 0.8 - 0.6 * math.exp(-0.3 * l)
        qkv_nat = _proj(h, w_in, l, tm=1024, tn=W_B, n_out=len(NAT_TILES) * W_B,
                        col_tile=_tile_lookup(NAT_TILES), out_dtype=BF16,
                        lead_tiles=W_A // W_B, lead_scale=D_A ** -0.5 * LOG2E).reshape(b, s, -1)
        qkv_str = _proj(h, w_in, l, tm=1024, tn=W_B, n_out=len(STR_TILES) * W_B,
                        col_tile=_tile_lookup(STR_TILES), out_dtype=F32).reshape(b, s, -1)
        mem_kv = _proj(mem_n, w_mem_kv, l, tm=b * MEM_LEN, tn=W_C, n_out=2 * W_C,
                       col_tile=lambda j: j, out_dtype=BF16).reshape(b, MEM_LEN, 2 * W_C)
        o_a = _attn_a(qkv_nat, diff_lambda, subln, e_a, l, lam_init).reshape(b * s, W_A)
        o_b, o_c = (o.reshape(b * s, -1) for o in _attn_bc(qkv_nat, qkv_str, mem_kv, t_b))
        merged = _merge(h, o_a, o_b, o_c, w_gate, b_gate, w_proj_a, w_proj_b, w_proj_c, l, tm=512, tn=512)
        x1, h2 = _out_proj(xf, merged, w_out, ffn_norm, l, tm=512)
        act = _ffn_act(h2, w_ffn_gate, w_ffn_up, l, tm=1024, tf=512)
        if l == depth - 1:
            (xf,) = _ffn_down(x1, act, w_ffn_down, final_norm, l, 0, True, tm=256)
        else:
            xf, h = _ffn_down(x1, act, w_ffn_down, attn_norm, l, l + 1, False, tm=256)
    return xf.reshape(b, s, d)
```

```python
import functools
import math

import jax
import jax.numpy as jnp
from jax import lax
from jax.experimental import pallas as pl
from jax.experimental.pallas import tpu as pltpu

F32 = jnp.float32
BF16 = jnp.bfloat16

D_MODEL = 2048
MEM_LEN = 256
EPS = 1e-5
NEG_INF = -1e30
H_A = 8
D_A = 64
W_A = H_A * 2 * D_A
DILATED_GROUPS = ((128, 1), (512, 4), (2048, 16))
N_GROUPS = 3
HB = 4
D_B = 128
W_BQKV = N_GROUPS * HB * D_B
W_B = HB * D_B
H_C = 4
D_C = 128
W_C = H_C * D_C
N_IN = 3 * W_A + 3 * W_BQKV + W_C
NUM_BUCKETS = 32
REL_MAX_DISTANCE = 1024

LOG2E = math.log2(math.e)
TQ = 128
A_CHUNK = 512
C_ROWS = 512
TL = 128
HALF = 64
LANES = 128
VMEM_LIMIT = 56 * 1024 * 1024

N_TILES_A = 3 * W_A // W_B
NAT_TILES = tuple(N_TILES_A + 3 * t for t in range(4))
STR_TILES = tuple(N_TILES_A + 3 * t + g for t in range(3) for g in (1, 2))
NAT_QB0, NAT_KB0, NAT_VB0, NAT_QC = range(4)

_NT = (((1,), (1,)), ((), ()))


def _rms(xf, g):
    return xf * lax.rsqrt(jnp.mean(xf * xf, axis=-1, keepdims=True) + EPS) * g


def _dot(a, b):
    return jnp.dot(a, b, preferred_element_type=F32)


def _cparams(sem):
    return pltpu.CompilerParams(dimension_semantics=sem, vmem_limit_bytes=VMEM_LIMIT)


def _rel_bucket(rel):
    half_b = NUM_BUCKETS // 2
    max_exact = half_b // 2
    n = jnp.abs(rel)
    nf = jnp.maximum(n, 1).astype(F32)
    large = max_exact + (jnp.log(nf / max_exact) / math.log(REL_MAX_DISTANCE / max_exact)
                         * (half_b - max_exact)).astype(jnp.int32)
    large = jnp.minimum(large, half_b - 1)
    return jnp.where(rel > 0, half_b, 0) + jnp.where(n < max_exact, n, large)


def _lookup(bucket, tab_ref, col):
    out = jnp.zeros(bucket.shape, F32)
    for b in range(NUM_BUCKETS):
        out = jnp.where(bucket == b, tab_ref[b, col], out)
    return out


def _bias_a_kernel(tab_ref, e_ref, *, nq):
    j = pl.program_id(0)
    kk = lax.broadcasted_iota(jnp.int32, (TQ, TQ), 0)
    r = lax.broadcasted_iota(jnp.int32, (TQ, TQ), 1)
    bucket = _rel_bucket((j - (nq - 1)) * TQ + kk - r)
    for h in range(H_A):
        for m in range(2):
            e_ref[h, :, m * TQ:(m + 1) * TQ] = _lookup(bucket, tab_ref, m * H_A + h) * LOG2E


def _bias_a(rel_bias, nq):
    nt = 2 * nq - 1
    return pl.pallas_call(
        functools.partial(_bias_a_kernel, nq=nq),
        out_shape=jax.ShapeDtypeStruct((H_A, nt, TQ, 2 * TQ), F32),
        grid=(nt,),
        in_specs=[pl.BlockSpec(memory_space=pltpu.SMEM)],
        out_specs=pl.BlockSpec((H_A, None, TQ, 2 * TQ), lambda j: (0, j, 0, 0)),
        compiler_params=_cparams(("parallel",)),
        name="bias_a",
    )(rel_bias)


def _halo(cls_len):
    return 0 if cls_len == TL else HALF


def _bias_b_kernel(tab_ref, t_ref, *, seq):
    g = pl.program_id(0)
    r = lax.broadcasted_iota(jnp.int32, (TL, 2 * TL), 0)
    kk = lax.broadcasted_iota(jnp.int32, (TL, 2 * TL), 1)
    halo, dil = _halo(seq // DILATED_GROUPS[0][1]), DILATED_GROUPS[0][1]
    for gg in range(1, N_GROUPS):
        halo = jnp.where(g == gg, _halo(seq // DILATED_GROUPS[gg][1]), halo)
        dil = jnp.where(g == gg, DILATED_GROUPS[gg][1], dil)
    step = kk - halo - r
    bucket = _rel_bucket(step * dil)
    for hh in range(HB):
        t_ref[hh] = jnp.where(jnp.abs(step) <= HALF, _lookup(bucket, tab_ref, 2 * H_A + g * HB + hh), NEG_INF)


def _bias_b(rel_bias, seq):
    return pl.pallas_call(
        functools.partial(_bias_b_kernel, seq=seq),
        out_shape=jax.ShapeDtypeStruct((N_GROUPS, HB, TL, 2 * TL), F32),
        grid=(N_GROUPS,),
        in_specs=[pl.BlockSpec(memory_space=pltpu.SMEM)],
        out_specs=pl.BlockSpec((None, HB, TL, 2 * TL), lambda g: (g, 0, 0, 0)),
        compiler_params=_cparams(("parallel",)),
        name="bias_b",
    )(rel_bias)


def _norm_kernel(x_ref, g_ref, o_ref):
    o_ref[...] = _rms(x_ref[...], g_ref[...]).astype(o_ref.dtype)


def _norm(x, gains, lg, *, tm):
    m, d = x.shape
    return pl.pallas_call(
        _norm_kernel,
        out_shape=jax.ShapeDtypeStruct((m, d), BF16),
        grid=(m // tm,),
        in_specs=[pl.BlockSpec((tm, d), lambda i: (i, 0)), pl.BlockSpec((None, 1, d), lambda i: (lg, 0, 0))],
        out_specs=pl.BlockSpec((tm, d), lambda i: (i, 0)),
        compiler_params=_cparams(("parallel",)),
        name="norm",
    )(x, gains)


def _proj_kernel(h_ref, w_ref, o_ref, w_scr, *, lead_tiles, lead_scale):
    j = pl.program_id(0)

    @pl.when(pl.program_id(1) == 0)
    def _():
        w_scr[...] = w_ref[...].astype(BF16)

    acc = _dot(h_ref[...], w_scr[...])
    if lead_tiles:
        acc = acc * jnp.where(j < lead_tiles, lead_scale, 1.0)
    o_ref[...] = acc.astype(o_ref.dtype)


def _tile_lookup(tiles):
    def f(j):
        out = tiles[0]
        for idx, t in enumerate(tiles[1:], 1):
            out = jnp.where(j == idx, t, out)
        return out
    return f


def _proj(h, w, l, *, tm, tn, n_out, col_tile, out_dtype, lead_tiles=0, lead_scale=1.0):
    m, k = h.shape
    return pl.pallas_call(
        functools.partial(_proj_kernel, lead_tiles=lead_tiles, lead_scale=lead_scale),
        out_shape=jax.ShapeDtypeStruct((m, n_out), out_dtype),
        grid=(n_out // tn, m // tm),
        in_specs=[
            pl.BlockSpec((tm, k), lambda j, i: (i, 0)),
            pl.BlockSpec((None, k, tn), lambda j, i: (l, 0, col_tile(j))),
        ],
        out_specs=pl.BlockSpec((tm, tn), lambda j, i: (i, j)),
        scratch_shapes=[pltpu.VMEM((k, tn), BF16)],
        compiler_params=_cparams(("parallel", "arbitrary")),
        name="proj",
    )(h, w)


def _attn_a_kernel(dl_ref, q_ref, k_ref, v_ref, e_ref, sg_ref, o_ref, s_scr, p_scr, *, lam_init, nq):
    seq = k_ref.shape[0]
    nch = seq // A_CHUNK
    tiles_per_chunk = A_CHUNK // TQ
    dl = dl_ref[...]
    lam = (jnp.exp(jnp.sum(dl[0:1] * dl[1:2], axis=-1, keepdims=True))
           - jnp.exp(jnp.sum(dl[2:3] * dl[3:4], axis=-1, keepdims=True)) + lam_init)
    lane = lax.broadcasted_iota(jnp.int32, (TQ, 2 * D_A), 1)
    zero = jnp.zeros((TQ, 2 * D_A), BF16)
    qq, col_max, col_sum = {}, {}, {}

    def part_reduce(x, op):
        return op(x.reshape(A_CHUNK // 8, 8, 2 * TQ), axis=0)

    def stage_a(t, c):
        if c == 0:
            q = q_ref[t * TQ:(t + 1) * TQ, :]
            qq[t] = jnp.concatenate([jnp.where(lane < D_A, q, zero), jnp.where(lane >= D_A, q, zero)], axis=0)
        keys = slice(c * A_CHUNK, (c + 1) * A_CHUNK)
        s = lax.dot_general(k_ref[keys, :], qq[t], _NT, preferred_element_type=F32)
        j0 = c * tiles_per_chunk - t + nq - 1
        s = s + e_ref[j0:j0 + tiles_per_chunk].reshape(A_CHUNK, 2 * TQ)
        s_scr[t % 2, keys, :] = s
        part = part_reduce(s, jnp.max)
        col_max[t] = part if c == 0 else jnp.maximum(col_max[t], part)
        if c == nch - 1:
            col_max[t] = jnp.max(col_max[t], axis=0, keepdims=True)
            del qq[t]

    def stage_b(t, c):
        keys = slice(c * A_CHUNK, (c + 1) * A_CHUNK)
        p = jnp.exp2(s_scr[t % 2, keys, :] - col_max[t])
        p_scr[t % 2, keys, :] = p.astype(BF16)
        part = part_reduce(p, jnp.sum)
        col_sum[t] = part if c == 0 else col_sum[t] + part
        if c == nch - 1:
            col_sum[t] = jnp.sum(col_sum[t], axis=0, keepdims=True)
            del col_max[t]

    def stage_c(t):
        ot = lax.dot_general(v_ref[...], p_scr[t % 2], (((0,), (0,)), ((), ())), preferred_element_type=F32)
        rinv = 1.0 / col_sum.pop(t)
        ot = ot[:, :TQ] * rinv[:, :TQ] - ot[:, TQ:] * (lam * rinv[:, TQ:])
        o = ot.T
        o_ref[t * TQ:(t + 1) * TQ, :] = (_rms(o, sg_ref[...]) * (1.0 - lam_init)).astype(o_ref.dtype)

    for step in range(nq + 2):
        for c in range(nch):
            if step < nq:
                stage_a(step, c)
            if 0 <= step - 1 < nq:
                stage_b(step - 1, c)
            if c == 0 and 0 <= step - 2 < nq:
                stage_c(step - 2)


def _attn_a(qkv, diff_lambda, subln, e_a, l, lam_init):
    b, s, _ = qkv.shape
    nq = s // TQ
    assert s % A_CHUNK == 0 and A_CHUNK % TQ == 0
    w = 2 * D_A
    return pl.pallas_call(
        functools.partial(_attn_a_kernel, lam_init=lam_init, nq=nq),
        out_shape=jax.ShapeDtypeStruct((b, s, W_A), BF16),
        grid=(H_A, b),
        in_specs=[
            pl.BlockSpec((None, 4, D_A), lambda h, bb: (l, 0, 0)),
            pl.BlockSpec((None, s, w), lambda h, bb: (bb, 0, h)),
            pl.BlockSpec((None, s, w), lambda h, bb: (bb, 0, W_A // w + h)),
            pl.BlockSpec((None, s, w), lambda h, bb: (bb, 0, 2 * W_A // w + h)),
            pl.BlockSpec((None, 2 * nq - 1, TQ, 2 * TQ), lambda h, bb: (h, 0, 0, 0)),
            pl.BlockSpec((None, 1, w), lambda h, bb: (l, 0, 0)),
        ],
        out_specs=pl.BlockSpec((None, s, w), lambda h, bb: (bb, 0, h)),
        scratch_shapes=[pltpu.VMEM((2, s, 2 * TQ), F32), pltpu.VMEM((2, s, 2 * TQ), BF16)],
        compiler_params=_cparams(("parallel", "parallel")),
        name="attn_a",
    )(diff_lambda, qkv, qkv, qkv, e_a, subln)


def _b_unit(q, kw, vw, bias, col_ok, acc_scr, m_scr, l_scr, rows, first):
    s = lax.dot_general(q, kw, _NT, preferred_element_type=F32) * (D_B ** -0.5) + bias
    if col_ok is not None:
        s = jnp.where(col_ok, s, NEG_INF)
    m_col = jnp.max(s, axis=-1, keepdims=True)
    if not first:
        m_old = m_scr[rows, :]
        m_col = jnp.maximum(m_old[:, :1], m_col)
        alpha = jnp.exp(m_old - m_col)
    e = jnp.exp(s - m_col)
    l_new = jnp.sum(e, axis=-1, keepdims=True)
    acc = _dot(e.astype(BF16), vw)
    if first:
        l_new = jnp.broadcast_to(l_new, (TL, LANES))
    else:
        l_new = l_scr[rows, :] * alpha + l_new
        acc = acc_scr[rows, :] * alpha + acc
    acc_scr[rows, :] = acc
    l_scr[rows, :] = l_new
    m_scr[rows, :] = jnp.broadcast_to(m_col, (TL, LANES))


def _attn_c(qc_ref, mk_ref, mv_ref, oc_ref):
    for r0 in range(0, qc_ref.shape[0], C_ROWS):
        rows = slice(r0, r0 + C_ROWS)
        s = lax.dot_general(qc_ref[rows, :], mk_ref[...], _NT, preferred_element_type=F32) * (D_C ** -0.5)
        e = jnp.exp(s - jnp.max(s, axis=-1, keepdims=True))
        p = e / jnp.sum(e, axis=-1, keepdims=True)
        oc_ref[rows, :] = _dot(p.astype(BF16), mv_ref[...]).astype(oc_ref.dtype)


def _attn_bc_kernel(*refs, seq):
    nat = refs[0:3]
    strided = refs[3:3 + 3 * (N_GROUPS - 1)]
    t_ref, qc_ref, mk_ref, mv_ref, o_ref, oc_ref = refs[-11:-5]
    cls_scrs = refs[-5:-3]
    acc_scr, m_scr, l_scr = refs[-3:]

    _attn_c(qc_ref, mk_ref, mv_ref, oc_ref)

    kk = lax.broadcasted_iota(jnp.int32, (TL, 2 * TL), 1)
    for g, (_, dil) in enumerate(DILATED_GROUPS):
        cls_len = seq // dil
        halo = _halo(cls_len)
        if dil == 1:
            srcs = [(lambda ref: (lambda sl: ref[sl, :]))(ref) for ref in nat]
        else:
            scr = cls_scrs[g - 1]
            for t in range(3):
                for r in range(dil):
                    scr[t, r] = strided[3 * (g - 1) + t][pl.ds(r, cls_len, stride=dil), :].astype(BF16)
        for r in range(dil):
            if dil != 1:
                srcs = [(lambda t: (lambda sl: scr[t, r, sl, :]))(t) for t in range(3)]
            q_at, k_at, v_at = srcs
            for m0 in range(0, cls_len, TL):
                mid = pl.ds(m0, TL)
                if halo:
                    left = pl.ds((m0 - halo) % cls_len, halo)
                    right = pl.ds((m0 + TL) % cls_len, halo)
                    kw = jnp.concatenate([k_at(left), k_at(mid), k_at(right)], axis=0)
                    vw = jnp.concatenate([v_at(left), v_at(mid), v_at(right)], axis=0)
                    bias = t_ref[g]
                    col_ok = None
                    if m0 == 0:
                        col_ok = kk >= halo
                    if m0 + TL == cls_len:
                        col_ok = kk < halo + TL if col_ok is None else col_ok & (kk < halo + TL)
                else:
                    kw, vw, bias, col_ok = k_at(mid), v_at(mid), t_ref[g, :, :TL], None
                rows = pl.ds(r + dil * m0, TL, stride=dil) if dil != 1 else mid
                _b_unit(q_at(mid), kw, vw, bias, col_ok, acc_scr, m_scr, l_scr, rows, first=(g == 0))
    o_ref[...] = (acc_scr[...] / l_scr[...]).astype(o_ref.dtype)


def _attn_bc(qkv_nat, qkv_str, mem_kv, t_b):
    b, s, _ = qkv_nat.shape
    assert DILATED_GROUPS[0][1] == 1 and HB == H_C and D_B == D_C and s % C_ROWS == 0
    blk = lambda tile: pl.BlockSpec((None, s, D_B), lambda bb, hh: (bb, 0, tile * HB + hh))
    in_specs = [blk(NAT_QB0), blk(NAT_KB0), blk(NAT_VB0)]
    args = [qkv_nat] * 3
    for g in range(1, N_GROUPS):
        for t in range(3):
            in_specs.append(blk(STR_TILES.index(N_TILES_A + 3 * t + g)))
            args.append(qkv_str)
    in_specs += [
        pl.BlockSpec((N_GROUPS, None, TL, 2 * TL), lambda bb, hh: (0, hh, 0, 0)),
        blk(NAT_QC),
        pl.BlockSpec((None, MEM_LEN, D_C), lambda bb, hh: (bb, 0, hh)),
        pl.BlockSpec((None, MEM_LEN, D_C), lambda bb, hh: (bb, 0, H_C + hh)),
    ]
    args += [t_b, qkv_nat, mem_kv, mem_kv]
    cls_scrs = [pltpu.VMEM((3, dil, s // dil, D_B), BF16) for _, dil in DILATED_GROUPS[1:]]
    out = jax.ShapeDtypeStruct((b, s, W_B), BF16)
    out_spec = pl.BlockSpec((None, s, D_B), lambda bb, hh: (bb, 0, hh))
    return pl.pallas_call(
        functools.partial(_attn_bc_kernel, seq=s),
        out_shape=[out, out],
        grid=(b, HB),
        in_specs=in_specs,
        out_specs=[out_spec, out_spec],
        scratch_shapes=cls_scrs + [pltpu.VMEM((s, LANES), F32)] * 3,
        compiler_params=_cparams(("parallel", "parallel")),
        name="attn_bc",
    )(*args)


def _merge_kernel(h_ref, oa_ref, ob_ref, oc_ref, wg0_ref, wg1_ref, wg2_ref, bg_ref, wpa_ref, wpb_ref, wpc_ref,
                  o_ref, wg_scr, wpa_scr, wpb_scr, wpc_scr):
    @pl.when(pl.program_id(1) == 0)
    def _():
        for br, w_ref in enumerate((wg0_ref, wg1_ref, wg2_ref)):
            wg_scr[br] = w_ref[...].astype(BF16)
        wpa_scr[...] = wpa_ref[...].astype(BF16)
        wpb_scr[...] = wpb_ref[...].astype(BF16)
        wpc_scr[...] = wpc_ref[...].astype(BF16)

    h = h_ref[...]

    def gate(br):
        return jax.nn.sigmoid(_dot(h, wg_scr[br]) + bg_ref[br])

    merged = (gate(0) * _dot(oa_ref[...], wpa_scr[...])
              + gate(1) * _dot(ob_ref[...], wpb_scr[...])
              + gate(2) * _dot(oc_ref[...], wpc_scr[...]))
    o_ref[...] = merged.astype(o_ref.dtype)


def _merge(h, o_a, o_b, o_c, w_gate, b_gate, w_pa, w_pb, w_pc, l, *, tm, tn):
    m, d = h.shape
    nd = d // tn

    def wg_spec(br):
        return pl.BlockSpec((None, d, tn), lambda j, i: (l, 0, br * nd + j))

    def rows(width):
        return pl.BlockSpec((tm, width), lambda j, i: (i, 0))

    def wp_spec(width):
        return pl.BlockSpec((None, width, tn), lambda j, i: (l, 0, j))

    return pl.pallas_call(
        _merge_kernel,
        out_shape=jax.ShapeDtypeStruct((m, d), BF16),
        grid=(nd, m // tm),
        in_specs=[
            rows(d), rows(W_A), rows(W_B), rows(W_C),
            wg_spec(0), wg_spec(1), wg_spec(2),
            pl.BlockSpec((None, 3, 1, tn), lambda j, i: (l, 0, 0, j)),
            wp_spec(W_A), wp_spec(W_B), wp_spec(W_C),
        ],
        out_specs=pl.BlockSpec((tm, tn), lambda j, i: (i, j)),
        scratch_shapes=[pltpu.VMEM((3, d, tn), BF16), pltpu.VMEM((W_A, tn), BF16),
                        pltpu.VMEM((W_B, tn), BF16), pltpu.VMEM((W_C, tn), BF16)],
        compiler_params=_cparams(("parallel", "arbitrary")),
        name="merge",
    )(h, o_a, o_b, o_c, w_gate, w_gate, w_gate, b_gate, w_pa, w_pb, w_pc)


def _out_proj_kernel(x_ref, m_ref, w_ref, g_ref, x1_ref, h2_ref):
    y = x_ref[...] + _dot(m_ref[...], w_ref[...])
    x1_ref[...] = y
    h2_ref[...] = _rms(y, g_ref[...]).astype(h2_ref.dtype)


def _out_proj(x, merged, w_out, gains, l, *, tm):
    m, d = x.shape
    row = pl.BlockSpec((tm, d), lambda i: (i, 0))
    return pl.pallas_call(
        _out_proj_kernel,
        out_shape=[jax.ShapeDtypeStruct((m, d), F32), jax.ShapeDtypeStruct((m, d), BF16)],
        grid=(m // tm,),
        in_specs=[row, row, pl.BlockSpec((None, d, d), lambda i: (l, 0, 0)),
                  pl.BlockSpec((None, 1, d), lambda i: (l, 0, 0))],
        out_specs=[row, row],
        compiler_params=_cparams(("parallel",)),
        name="out_proj",
    )(x, merged, w_out, gains)


def _ffn_act_kernel(h_ref, wg_ref, wu_ref, o_ref, wg_scr, wu_scr):
    @pl.when(pl.program_id(1) == 0)
    def _():
        wg_scr[...] = wg_ref[...].astype(BF16)
        wu_scr[...] = wu_ref[...].astype(BF16)

    h = h_ref[...]
    gt = _dot(h, wg_scr[...])
    o_ref[...] = (gt * jax.nn.sigmoid(gt) * _dot(h, wu_scr[...])).astype(o_ref.dtype)


def _ffn_act(h, w_g, w_u, l, *, tm, tf):
    m, d = h.shape
    f = w_g.shape[-1]
    w_spec = pl.BlockSpec((None, d, tf), lambda j, i: (l, 0, j))
    return pl.pallas_call(
        _ffn_act_kernel,
        out_shape=jax.ShapeDtypeStruct((m, f), BF16),
        grid=(f // tf, m // tm),
        in_specs=[pl.BlockSpec((tm, d), lambda j, i: (i, 0)), w_spec, w_spec],
        out_specs=pl.BlockSpec((tm, tf), lambda j, i: (i, j)),
        scratch_shapes=[pltpu.VMEM((d, tf), BF16)] * 2,
        compiler_params=_cparams(("parallel", "arbitrary")),
        name="ffn_act",
    )(h, w_g, w_u)


def _ffn_down_kernel(x_ref, a_ref, w_ref, g_ref, *out_refs, final):
    y = x_ref[...] + _dot(a_ref[...], w_ref[...])
    if final:
        out_refs[0][...] = _rms(y, g_ref[...])
    else:
        out_refs[0][...] = y
        out_refs[1][...] = _rms(y, g_ref[...]).astype(out_refs[1].dtype)


def _ffn_down(x, act, w_d, gains, l, lg, final, *, tm):
    m, d = x.shape
    f = act.shape[-1]
    row = pl.BlockSpec((tm, d), lambda i: (i, 0))
    out_shape = [jax.ShapeDtypeStruct((m, d), F32)]
    if not final:
        out_shape.append(jax.ShapeDtypeStruct((m, d), BF16))
    return pl.pallas_call(
        functools.partial(_ffn_down_kernel, final=final),
        out_shape=out_shape,
        grid=(m // tm,),
        in_specs=[row, pl.BlockSpec((tm, f), lambda i: (i, 0)),
                  pl.BlockSpec((None, f, d), lambda i: (l, 0, 0), pipeline_mode=pl.Buffered(1)),
                  pl.BlockSpec((None, 1, d), lambda i: (lg, 0, 0))],
        out_specs=[row] * len(out_shape),
        compiler_params=_cparams(("parallel",)),
        name="ffn_down",
    )(x, act, w_d, gains)


def kernel(x, mem, rel_bias, mem_norm, attn_norm, w_in, diff_lambda, diff_subln, w_mem_kv, w_gate, b_gate,
           w_proj_a, w_proj_b, w_proj_c, w_out, ffn_norm, w_ffn_gate, w_ffn_up, w_ffn_down, final_norm):
    b, s, d = x.shape
    depth = w_in.shape[0]
    assert d == D_MODEL and s % TQ == 0
    assert all(win // (2 * dil) == HALF and (s // dil) % TL == 0 for win, dil in DILATED_GROUPS)
    xf = x.reshape(b * s, d)
    memf = mem.reshape(b * MEM_LEN, d)

    w_out = w_out.astype(BF16)
    w_ffn_down = w_ffn_down.astype(BF16)
    attn_norm = attn_norm.reshape(depth, 1, d)
    ffn_norm = ffn_norm.reshape(depth, 1, d)
    mem_norm = mem_norm.reshape(1, 1, d)
    final_norm = final_norm.reshape(1, 1, d)
    subln = diff_subln.reshape(depth, 1, 2 * D_A)
    b_gate = b_gate.reshape(depth, 3, 1, d)

    e_a = _bias_a(rel_bias, s // TQ)
    t_b = _bias_b(rel_bias, s)
    mem_n = _norm(memf, mem_norm, 0, tm=512)
    h = _norm(xf, attn_norm, 0, tm=512)

    for l in range(depth):
        lam_init = 0.8 - 0.6 * math.exp(-0.3 * l)
        qkv_a = _proj(h, w_in, l, tm=1024, tn=W_A, n_out=3 * W_A, col_tile=lambda j: j, out_dtype=BF16,
                      lead_tiles=1, lead_scale=D_A ** -0.5 * LOG2E).reshape(b, s, -1)
        qkv_nat = _proj(h, w_in, l, tm=1024, tn=W_B, n_out=len(NAT_TILES) * W_B,
                        col_tile=_tile_lookup(NAT_TILES), out_dtype=BF16).reshape(b, s, -1)
        qkv_str = _proj(h, w_in, l, tm=1024, tn=W_B, n_out=len(STR_TILES) * W_B,
                        col_tile=_tile_lookup(STR_TILES), out_dtype=F32).reshape(b, s, -1)
        mem_kv = _proj(mem_n, w_mem_kv, l, tm=b * MEM_LEN, tn=W_C, n_out=2 * W_C,
                       col_tile=lambda j: j, out_dtype=BF16).reshape(b, MEM_LEN, 2 * W_C)
        o_a = _attn_a(qkv_a, diff_lambda, subln, e_a, l, lam_init).reshape(b * s, W_A)
        o_b, o_c = (o.reshape(b * s, -1) for o in _attn_bc(qkv_nat, qkv_str, mem_kv, t_b))
        merged = _merge(h, o_a, o_b, o_c, w_gate, b_gate, w_proj_a, w_proj_b, w_proj_c, l, tm=512, tn=512)
        x1, h2 = _out_proj(xf, merged, w_out, ffn_norm, l, tm=512)
        act = _ffn_act(h2, w_ffn_gate, w_ffn_up, l, tm=1024, tf=512)
        if l == depth - 1:
            (xf,) = _ffn_down(x1, act, w_ffn_down, final_norm, l, 0, True, tm=256)
        else:
            xf, h = _ffn_down(x1, act, w_ffn_down, attn_norm, l, l + 1, False, tm=256)
    return xf.reshape(b, s, d)
```

```python
import functools
import math

import jax
import jax.numpy as jnp
from jax import lax
from jax.experimental import pallas as pl
from jax.experimental.pallas import tpu as pltpu

F32 = jnp.float32
BF16 = jnp.bfloat16

D_MODEL = 2048
MEM_LEN = 256
EPS = 1e-5
NEG_INF = -1e30
H_A = 8
D_A = 64
W_A = H_A * 2 * D_A
DILATED_GROUPS = ((128, 1), (512, 4), (2048, 16))
N_GROUPS = 3
HB = 4
D_B = 128
W_BQKV = N_GROUPS * HB * D_B
W_B = HB * D_B
H_C = 4
D_C = 128
W_C = H_C * D_C
N_IN = 3 * W_A + 3 * W_BQKV + W_C
NUM_BUCKETS = 32
REL_MAX_DISTANCE = 1024

LOG2E = math.log2(math.e)
TQ = 128
A_CHUNK = 512
C_ROWS = 512
TL = 128
HALF = 64
LANES = 128
VMEM_LIMIT = 56 * 1024 * 1024

N_TILES_A = 3 * W_A // W_B
NAT_TILES = tuple(range(N_TILES_A)) + tuple(N_TILES_A + 3 * t for t in range(4))
STR_TILES = tuple(N_TILES_A + 3 * t + g for t in range(3) for g in (1, 2))
NAT_QB0, NAT_KB0, NAT_VB0, NAT_QC = (N_TILES_A + t for t in range(4))

_NT = (((1,), (1,)), ((), ()))


def _rms(xf, g):
    return xf * lax.rsqrt(jnp.mean(xf * xf, axis=-1, keepdims=True) + EPS) * g


def _dot(a, b):
    return jnp.dot(a, b, preferred_element_type=F32)


def _cparams(sem):
    return pltpu.CompilerParams(dimension_semantics=sem, vmem_limit_bytes=VMEM_LIMIT)


def _rel_bucket(rel):
    half_b = NUM_BUCKETS // 2
    max_exact = half_b // 2
    n = jnp.abs(rel)
    nf = jnp.maximum(n, 1).astype(F32)
    large = max_exact + (jnp.log(nf / max_exact) / math.log(REL_MAX_DISTANCE / max_exact)
                         * (half_b - max_exact)).astype(jnp.int32)
    large = jnp.minimum(large, half_b - 1)
    return jnp.where(rel > 0, half_b, 0) + jnp.where(n < max_exact, n, large)


def _lookup(bucket, tab_ref, col):
    out = jnp.zeros(bucket.shape, F32)
    for b in range(NUM_BUCKETS):
        out = jnp.where(bucket == b, tab_ref[b, col], out)
    return out


def _bias_a_kernel(tab_ref, e_ref, *, nq):
    j = pl.program_id(0)
    kk = lax.broadcasted_iota(jnp.int32, (TQ, TQ), 0)
    r = lax.broadcasted_iota(jnp.int32, (TQ, TQ), 1)
    bucket = _rel_bucket((j - (nq - 1)) * TQ + kk - r)
    bf = bucket.astype(F32)
    b_lo = jnp.min(bf).astype(jnp.int32)
    b_hi = jnp.max(bf).astype(jnp.int32)
    for h in range(H_A):
        for m in range(2):
            col = m * H_A + h
            tile = lax.fori_loop(b_lo, b_hi + 1,
                                 lambda b, out: jnp.where(bucket == b, tab_ref[b, col], out),
                                 jnp.zeros((TQ, TQ), F32))
            e_ref[h, :, m * TQ:(m + 1) * TQ] = tile * LOG2E


def _bias_a(rel_bias, nq):
    nt = 2 * nq - 1
    return pl.pallas_call(
        functools.partial(_bias_a_kernel, nq=nq),
        out_shape=jax.ShapeDtypeStruct((H_A, nt, TQ, 2 * TQ), F32),
        grid=(nt,),
        in_specs=[pl.BlockSpec(memory_space=pltpu.SMEM)],
        out_specs=pl.BlockSpec((H_A, None, TQ, 2 * TQ), lambda j: (0, j, 0, 0)),
        compiler_params=_cparams(("parallel",)),
        name="bias_a",
    )(rel_bias)


def _halo(cls_len):
    return 0 if cls_len == TL else HALF


def _bias_b_kernel(tab_ref, t_ref, *, seq):
    g = pl.program_id(0)
    r = lax.broadcasted_iota(jnp.int32, (TL, 2 * TL), 0)
    kk = lax.broadcasted_iota(jnp.int32, (TL, 2 * TL), 1)
    halo, dil = _halo(seq // DILATED_GROUPS[0][1]), DILATED_GROUPS[0][1]
    for gg in range(1, N_GROUPS):
        halo = jnp.where(g == gg, _halo(seq // DILATED_GROUPS[gg][1]), halo)
        dil = jnp.where(g == gg, DILATED_GROUPS[gg][1], dil)
    step = kk - halo - r
    bucket = _rel_bucket(step * dil)
    for hh in range(HB):
        t_ref[hh] = jnp.where(jnp.abs(step) <= HALF, _lookup(bucket, tab_ref, 2 * H_A + g * HB + hh), NEG_INF)


def _bias_b(rel_bias, seq):
    return pl.pallas_call(
        functools.partial(_bias_b_kernel, seq=seq),
        out_shape=jax.ShapeDtypeStruct((N_GROUPS, HB, TL, 2 * TL), F32),
        grid=(N_GROUPS,),
        in_specs=[pl.BlockSpec(memory_space=pltpu.SMEM)],
        out_specs=pl.BlockSpec((None, HB, TL, 2 * TL), lambda g: (g, 0, 0, 0)),
        compiler_params=_cparams(("parallel",)),
        name="bias_b",
    )(rel_bias)


def _norm_kernel(x_ref, g_ref, o_ref):
    o_ref[...] = _rms(x_ref[...], g_ref[...]).astype(o_ref.dtype)


def _norm(x, gains, lg, *, tm):
    m, d = x.shape
    return pl.pallas_call(
        _norm_kernel,
        out_shape=jax.ShapeDtypeStruct((m, d), BF16),
        grid=(m // tm,),
        in_specs=[pl.BlockSpec((tm, d), lambda i: (i, 0)), pl.BlockSpec((None, 1, d), lambda i: (lg, 0, 0))],
        out_specs=pl.BlockSpec((tm, d), lambda i: (i, 0)),
        compiler_params=_cparams(("parallel",)),
        name="norm",
    )(x, gains)


def _proj_kernel(h_ref, w_ref, o_ref, w_scr, *, lead_tiles, lead_scale):
    j = pl.program_id(0)

    @pl.when(pl.program_id(1) == 0)
    def _():
        w_scr[...] = w_ref[...].astype(BF16)

    acc = _dot(h_ref[...], w_scr[...])
    if lead_tiles:
        acc = acc * jnp.where(j < lead_tiles, lead_scale, 1.0)
    o_ref[...] = acc.astype(o_ref.dtype)


def _tile_lookup(tiles):
    def f(j):
        out = tiles[0]
        for idx, t in enumerate(tiles[1:], 1):
            out = jnp.where(j == idx, t, out)
        return out
    return f


def _proj(h, w, l, *, tm, tn, n_out, col_tile, out_dtype, lead_tiles=0, lead_scale=1.0):
    m, k = h.shape
    return pl.pallas_call(
        functools.partial(_proj_kernel, lead_tiles=lead_tiles, lead_scale=lead_scale),
        out_shape=jax.ShapeDtypeStruct((m, n_out), out_dtype),
        grid=(n_out // tn, m // tm),
        in_specs=[
            pl.BlockSpec((tm, k), lambda j, i: (i, 0)),
            pl.BlockSpec((None, k, tn), lambda j, i: (l, 0, col_tile(j))),
        ],
        out_specs=pl.BlockSpec((tm, tn), lambda j, i: (i, j)),
        scratch_shapes=[pltpu.VMEM((k, tn), BF16)],
        compiler_params=_cparams(("parallel", "arbitrary")),
        name="proj",
    )(h, w)


def _attn_a_kernel(dl_ref, q_ref, k_ref, v_ref, e_ref, sg_ref, o_ref, s_scr, p_scr, *, lam_init, nq):
    seq = k_ref.shape[0]
    nch = seq // A_CHUNK
    tiles_per_chunk = A_CHUNK // TQ
    dl = dl_ref[...]
    lam = (jnp.exp(jnp.sum(dl[0:1] * dl[1:2], axis=-1, keepdims=True))
           - jnp.exp(jnp.sum(dl[2:3] * dl[3:4], axis=-1, keepdims=True)) + lam_init)
    lane = lax.broadcasted_iota(jnp.int32, (TQ, 2 * D_A), 1)
    zero = jnp.zeros((TQ, 2 * D_A), BF16)
    qq, col_max, col_sum = {}, {}, {}

    def part_reduce(x, op):
        return op(x.reshape(A_CHUNK // 8, 8, 2 * TQ), axis=0)

    def stage_a(t, c):
        if c == 0:
            q = q_ref[t * TQ:(t + 1) * TQ, :]
            qq[t] = jnp.concatenate([jnp.where(lane < D_A, q, zero), jnp.where(lane >= D_A, q, zero)], axis=0)
        keys = slice(c * A_CHUNK, (c + 1) * A_CHUNK)
        s = lax.dot_general(k_ref[keys, :], qq[t], _NT, preferred_element_type=F32)
        j0 = c * tiles_per_chunk - t + nq - 1
        s = s + e_ref[j0:j0 + tiles_per_chunk].reshape(A_CHUNK, 2 * TQ)
        s_scr[t % 2, keys, :] = s
        part = part_reduce(s, jnp.max)
        col_max[t] = part if c == 0 else jnp.maximum(col_max[t], part)
        if c == nch - 1:
            col_max[t] = jnp.max(col_max[t], axis=0, keepdims=True)
            del qq[t]

    def stage_b(t, c):
        keys = slice(c * A_CHUNK, (c + 1) * A_CHUNK)
        p = jnp.exp2(s_scr[t % 2, keys, :] - col_max[t])
        p_scr[t % 2, keys, :] = p.astype(BF16)
        part = part_reduce(p, jnp.sum)
        col_sum[t] = part if c == 0 else col_sum[t] + part
        if c == nch - 1:
            col_sum[t] = jnp.sum(col_sum[t], axis=0, keepdims=True)
            del col_max[t]

    def stage_c(t):
        ot = lax.dot_general(v_ref[...], p_scr[t % 2], (((0,), (0,)), ((), ())), preferred_element_type=F32)
        rinv = 1.0 / col_sum.pop(t)
        ot = ot[:, :TQ] * rinv[:, :TQ] - ot[:, TQ:] * (lam * rinv[:, TQ:])
        o = ot.T
        o_ref[t * TQ:(t + 1) * TQ, :] = (_rms(o, sg_ref[...]) * (1.0 - lam_init)).astype(o_ref.dtype)

    for step in range(nq + 2):
        for c in range(nch):
            if step < nq:
                stage_a(step, c)
            if 0 <= step - 1 < nq:
                stage_b(step - 1, c)
            if c == 0 and 0 <= step - 2 < nq:
                stage_c(step - 2)


def _attn_a(qkv, diff_lambda, subln, e_a, l, lam_init):
    b, s, _ = qkv.shape
    nq = s // TQ
    assert s % A_CHUNK == 0 and A_CHUNK % TQ == 0
    w = 2 * D_A
    return pl.pallas_call(
        functools.partial(_attn_a_kernel, lam_init=lam_init, nq=nq),
        out_shape=jax.ShapeDtypeStruct((b, s, W_A), BF16),
        grid=(H_A, b),
        in_specs=[
            pl.BlockSpec((None, 4, D_A), lambda h, bb: (l, 0, 0)),
            pl.BlockSpec((None, s, w), lambda h, bb: (bb, 0, h)),
            pl.BlockSpec((None, s, w), lambda h, bb: (bb, 0, W_A // w + h)),
            pl.BlockSpec((None, s, w), lambda h, bb: (bb, 0, 2 * W_A // w + h)),
            pl.BlockSpec((None, 2 * nq - 1, TQ, 2 * TQ), lambda h, bb: (h, 0, 0, 0)),
            pl.BlockSpec((None, 1, w), lambda h, bb: (l, 0, 0)),
        ],
        out_specs=pl.BlockSpec((None, s, w), lambda h, bb: (bb, 0, h)),
        scratch_shapes=[pltpu.VMEM((2, s, 2 * TQ), F32), pltpu.VMEM((2, s, 2 * TQ), BF16)],
        compiler_params=_cparams(("parallel", "parallel")),
        name="attn_a",
    )(diff_lambda, qkv, qkv, qkv, e_a, subln)


def _b_unit(q, kw, vw, bias, col_ok, acc_scr, m_scr, l_scr, rows, first):
    s = lax.dot_general(q, kw, _NT, preferred_element_type=F32) * (D_B ** -0.5) + bias
    if col_ok is not None:
        s = jnp.where(col_ok, s, NEG_INF)
    m_col = jnp.max(s, axis=-1, keepdims=True)
    if not first:
        m_old = m_scr[rows, :]
        m_col = jnp.maximum(m_old[:, :1], m_col)
        alpha = jnp.exp(m_old - m_col)
    e = jnp.exp(s - m_col)
    l_new = jnp.sum(e, axis=-1, keepdims=True)
    acc = _dot(e.astype(BF16), vw)
    if first:
        l_new = jnp.broadcast_to(l_new, (TL, LANES))
    else:
        l_new = l_scr[rows, :] * alpha + l_new
        acc = acc_scr[rows, :] * alpha + acc
    acc_scr[rows, :] = acc
    l_scr[rows, :] = l_new
    m_scr[rows, :] = jnp.broadcast_to(m_col, (TL, LANES))


def _attn_c(qc_ref, mk_ref, mv_ref, oc_ref):
    for r0 in range(0, qc_ref.shape[0], C_ROWS):
        rows = slice(r0, r0 + C_ROWS)
        s = lax.dot_general(qc_ref[rows, :], mk_ref[...], _NT, preferred_element_type=F32) * (D_C ** -0.5)
        e = jnp.exp(s - jnp.max(s, axis=-1, keepdims=True))
        p = e / jnp.sum(e, axis=-1, keepdims=True)
        oc_ref[rows, :] = _dot(p.astype(BF16), mv_ref[...]).astype(oc_ref.dtype)


def _attn_bc_kernel(*refs, seq):
    nat = refs[0:3]
    strided = refs[3:3 + 3 * (N_GROUPS - 1)]
    t_ref, qc_ref, mk_ref, mv_ref, o_ref, oc_ref = refs[-11:-5]
    cls_scrs = refs[-5:-3]
    acc_scr, m_scr, l_scr = refs[-3:]

    _attn_c(qc_ref, mk_ref, mv_ref, oc_ref)

    kk = lax.broadcasted_iota(jnp.int32, (TL, 2 * TL), 1)
    for g, (_, dil) in enumerate(DILATED_GROUPS):
        cls_len = seq // dil
        halo = _halo(cls_len)
        if dil == 1:
            srcs = [(lambda ref: (lambda sl: ref[sl, :]))(ref) for ref in nat]
        else:
            scr = cls_scrs[g - 1]
            for t in range(3):
                for r in range(dil):
                    scr[t, r] = strided[3 * (g - 1) + t][pl.ds(r, cls_len, stride=dil), :].astype(BF16)
        for r in range(dil):
            if dil != 1:
                srcs = [(lambda t: (lambda sl: scr[t, r, sl, :]))(t) for t in range(3)]
            q_at, k_at, v_at = srcs
            for m0 in range(0, cls_len, TL):
                mid = pl.ds(m0, TL)
                if halo:
                    left = pl.ds((m0 - halo) % cls_len, halo)
                    right = pl.ds((m0 + TL) % cls_len, halo)
                    kw = jnp.concatenate([k_at(left), k_at(mid), k_at(right)], axis=0)
                    vw = jnp.concatenate([v_at(left), v_at(mid), v_at(right)], axis=0)
                    bias = t_ref[g]
                    col_ok = None
                    if m0 == 0:
                        col_ok = kk >= halo
                    if m0 + TL == cls_len:
                        col_ok = kk < halo + TL if col_ok is None else col_ok & (kk < halo + TL)
                else:
                    kw, vw, bias, col_ok = k_at(mid), v_at(mid), t_ref[g, :, :TL], None
                rows = pl.ds(r + dil * m0, TL, stride=dil) if dil != 1 else mid
                _b_unit(q_at(mid), kw, vw, bias, col_ok, acc_scr, m_scr, l_scr, rows, first=(g == 0))
    o_ref[...] = (acc_scr[...] / l_scr[...]).astype(o_ref.dtype)


def _attn_bc(qkv_nat, qkv_str, mem_kv, t_b):
    b, s, _ = qkv_nat.shape
    assert DILATED_GROUPS[0][1] == 1 and HB == H_C and D_B == D_C and s % C_ROWS == 0
    blk = lambda tile: pl.BlockSpec((None, s, D_B), lambda bb, hh: (bb, 0, tile * HB + hh))
    in_specs = [blk(NAT_QB0), blk(NAT_KB0), blk(NAT_VB0)]
    args = [qkv_nat] * 3
    for g in range(1, N_GROUPS):
        for t in range(3):
            in_specs.append(blk(STR_TILES.index(N_TILES_A + 3 * t + g)))
            args.append(qkv_str)
    in_specs += [
        pl.BlockSpec((N_GROUPS, None, TL, 2 * TL), lambda bb, hh: (0, hh, 0, 0)),
        blk(NAT_QC),
        pl.BlockSpec((None, MEM_LEN, D_C), lambda bb, hh: (bb, 0, hh)),
        pl.BlockSpec((None, MEM_LEN, D_C), lambda bb, hh: (bb, 0, H_C + hh)),
    ]
    args += [t_b, qkv_nat, mem_kv, mem_kv]
    cls_scrs = [pltpu.VMEM((3, dil, s // dil, D_B), BF16) for _, dil in DILATED_GROUPS[1:]]
    out = jax.ShapeDtypeStruct((b, s, W_B), BF16)
    out_spec = pl.BlockSpec((None, s, D_B), lambda bb, hh: (bb, 0, hh))
    return pl.pallas_call(
        functools.partial(_attn_bc_kernel, seq=s),
        out_shape=[out, out],
        grid=(b, HB),
        in_specs=in_specs,
        out_specs=[out_spec, out_spec],
        scratch_shapes=cls_scrs + [pltpu.VMEM((s, LANES), F32)] * 3,
        compiler_params=_cparams(("parallel", "parallel")),
        name="attn_bc",
    )(*args)


def _merge_kernel(h_ref, oa_ref, ob_ref, oc_ref, wg0_ref, wg1_ref, wg2_ref, bg_ref, wpa_ref, wpb_ref, wpc_ref,
                  o_ref, wg_scr, wpa_scr, wpb_scr, wpc_scr):
    @pl.when(pl.program_id(1) == 0)
    def _():
        for br, w_ref in enumerate((wg0_ref, wg1_ref, wg2_ref)):
            wg_scr[br] = w_ref[...].astype(BF16)
        wpa_scr[...] = wpa_ref[...].astype(BF16)
        wpb_scr[...] = wpb_ref[...].astype(BF16)
        wpc_scr[...] = wpc_ref[...].astype(BF16)

    h = h_ref[...]

    def gate(br):
        return jax.nn.sigmoid(_dot(h, wg_scr[br]) + bg_ref[br])

    merged = (gate(0) * _dot(oa_ref[...], wpa_scr[...])
              + gate(1) * _dot(ob_ref[...], wpb_scr[...])
              + gate(2) * _dot(oc_ref[...], wpc_scr[...]))
    o_ref[...] = merged.astype(o_ref.dtype)


def _merge(h, o_a, o_b, o_c, w_gate, b_gate, w_pa, w_pb, w_pc, l, *, tm, tn):
    m, d = h.shape
    nd = d // tn

    def wg_spec(br):
        return pl.BlockSpec((None, d, tn), lambda j, i: (l, 0, br * nd + j))

    def rows(width):
        return pl.BlockSpec((tm, width), lambda j, i: (i, 0))

    def wp_spec(width):
        return pl.BlockSpec((None, width, tn), lambda j, i: (l, 0, j))

    return pl.pallas_call(
        _merge_kernel,
        out_shape=jax.ShapeDtypeStruct((m, d), BF16),
        grid=(nd, m // tm),
        in_specs=[
            rows(d), rows(W_A), rows(W_B), rows(W_C),
            wg_spec(0), wg_spec(1), wg_spec(2),
            pl.BlockSpec((None, 3, 1, tn), lambda j, i: (l, 0, 0, j)),
            wp_spec(W_A), wp_spec(W_B), wp_spec(W_C),
        ],
        out_specs=pl.BlockSpec((tm, tn), lambda j, i: (i, j)),
        scratch_shapes=[pltpu.VMEM((3, d, tn), BF16), pltpu.VMEM((W_A, tn), BF16),
                        pltpu.VMEM((W_B, tn), BF16), pltpu.VMEM((W_C, tn), BF16)],
        compiler_params=_cparams(("parallel", "arbitrary")),
        name="merge",
    )(h, o_a, o_b, o_c, w_gate, w_gate, w_gate, b_gate, w_pa, w_pb, w_pc)


def _out_proj_kernel(x_ref, m_ref, w_ref, g_ref, x1_ref, h2_ref):
    y = x_ref[...] + _dot(m_ref[...], w_ref[...])
    x1_ref[...] = y
    h2_ref[...] = _rms(y, g_ref[...]).astype(h2_ref.dtype)


def _out_proj(x, merged, w_out, gains, l, *, tm):
    m, d = x.shape
    row = pl.BlockSpec((tm, d), lambda i: (i, 0))
    return pl.pallas_call(
        _out_proj_kernel,
        out_shape=[jax.ShapeDtypeStruct((m, d), F32), jax.ShapeDtypeStruct((m, d), BF16)],
        grid=(m // tm,),
        in_specs=[row, row, pl.BlockSpec((None, d, d), lambda i: (l, 0, 0)),
                  pl.BlockSpec((None, 1, d), lambda i: (l, 0, 0))],
        out_specs=[row, row],
        compiler_params=_cparams(("parallel",)),
        name="out_proj",
    )(x, merged, w_out, gains)


def _ffn_act_kernel(h_ref, wg_ref, wu_ref, o_ref, wg_scr, wu_scr):
    @pl.when(pl.program_id(1) == 0)
    def _():
        wg_scr[...] = wg_ref[...].astype(BF16)
        wu_scr[...] = wu_ref[...].astype(BF16)

    h = h_ref[...]
    gt = _dot(h, wg_scr[...])
    o_ref[...] = (gt * jax.nn.sigmoid(gt) * _dot(h, wu_scr[...])).astype(o_ref.dtype)


def _ffn_act(h, w_g, w_u, l, *, tm, tf):
    m, d = h.shape
    f = w_g.shape[-1]
    w_spec = pl.BlockSpec((None, d, tf), lambda j, i: (l, 0, j))
    return pl.pallas_call(
        _ffn_act_kernel,
        out_shape=jax.ShapeDtypeStruct((m, f), BF16),
        grid=(f // tf, m // tm),
        in_specs=[pl.BlockSpec((tm, d), lambda j, i: (i, 0)), w_spec, w_spec],
        out_specs=pl.BlockSpec((tm, tf), lambda j, i: (i, j)),
        scratch_shapes=[pltpu.VMEM((d, tf), BF16)] * 2,
        compiler_params=_cparams(("parallel", "arbitrary")),
        name="ffn_act",
    )(h, w_g, w_u)


def _ffn_down_kernel(x_ref, a_ref, w_ref, g_ref, *out_refs, final):
    y = x_ref[...] + _dot(a_ref[...], w_ref[...])
    if final:
        out_refs[0][...] = _rms(y, g_ref[...])
    else:
        out_refs[0][...] = y
        out_refs[1][...] = _rms(y, g_ref[...]).astype(out_refs[1].dtype)


def _ffn_down(x, act, w_d, gains, l, lg, final, *, tm):
    m, d = x.shape
    f = act.shape[-1]
    row = pl.BlockSpec((tm, d), lambda i: (i, 0))
    out_shape = [jax.ShapeDtypeStruct((m, d), F32)]
    if not final:
        out_shape.append(jax.ShapeDtypeStruct((m, d), BF16))
    return pl.pallas_call(
        functools.partial(_ffn_down_kernel, final=final),
        out_shape=out_shape,
        grid=(m // tm,),
        in_specs=[row, pl.BlockSpec((tm, f), lambda i: (i, 0)),
                  pl.BlockSpec((None, f, d), lambda i: (l, 0, 0), pipeline_mode=pl.Buffered(1)),
                  pl.BlockSpec((None, 1, d), lambda i: (lg, 0, 0))],
        out_specs=[row] * len(out_shape),
        compiler_params=_cparams(("parallel",)),
        name="ffn_down",
    )(x, act, w_d, gains)


def kernel(x, mem, rel_bias, mem_norm, attn_norm, w_in, diff_lambda, diff_subln, w_mem_kv, w_gate, b_gate,
           w_proj_a, w_proj_b, w_proj_c, w_out, ffn_norm, w_ffn_gate, w_ffn_up, w_ffn_down, final_norm):
    b, s, d = x.shape
    depth = w_in.shape[0]
    assert d == D_MODEL and s % TQ == 0
    assert all(win // (2 * dil) == HALF and (s // dil) % TL == 0 for win, dil in DILATED_GROUPS)
    xf = x.reshape(b * s, d)
    memf = mem.reshape(b * MEM_LEN, d)

    w_out = w_out.astype(BF16)
    w_ffn_down = w_ffn_down.astype(BF16)
    attn_norm = attn_norm.reshape(depth, 1, d)
    ffn_norm = ffn_norm.reshape(depth, 1, d)
    mem_norm = mem_norm.reshape(1, 1, d)
    final_norm = final_norm.reshape(1, 1, d)
    subln = diff_subln.reshape(depth, 1, 2 * D_A)
    b_gate = b_gate.reshape(depth, 3, 1, d)

    e_a = _bias_a(rel_bias, s // TQ)
    t_b = _bias_b(rel_bias, s)
    mem_n = _norm(memf, mem_norm, 0, tm=512)
    h = _norm(xf, attn_norm, 0, tm=512)

    for l in range(depth):
        lam_init = 0.8 - 0.6 * math.exp(-0.3 * l)
        qkv_nat = _proj(h, w_in, l, tm=1024, tn=W_B, n_out=len(NAT_TILES) * W_B,
                        col_tile=_tile_lookup(NAT_TILES), out_dtype=BF16,
                        lead_tiles=W_A // W_B, lead_scale=D_A ** -0.5 * LOG2E).reshape(b, s, -1)
        qkv_str = _proj(h, w_in, l, tm=1024, tn=W_B, n_out=len(STR_TILES) * W_B,
                        col_tile=_tile_lookup(STR_TILES), out_dtype=F32).reshape(b, s, -1)
        mem_kv = _proj(mem_n, w_mem_kv, l, tm=b * MEM_LEN, tn=W_C, n_out=2 * W_C,
                       col_tile=lambda j: j, out_dtype=BF16).reshape(b, MEM_LEN, 2 * W_C)
        o_a = _attn_a(qkv_nat, diff_lambda, subln, e_a, l, lam_init).reshape(b * s, W_A)
        o_b, o_c = (o.reshape(b * s, -1) for o in _attn_bc(qkv_nat, qkv_str, mem_kv, t_b))
        merged = _merge(h, o_a, o_b, o_c, w_gate, b_gate, w_proj_a, w_proj_b, w_proj_c, l, tm=512, tn=512)
        x1, h2 = _out_proj(xf, merged, w_out, ffn_norm, l, tm=512)
        act = _ffn_act(h2, w_ffn_gate, w_ffn_up, l, tm=1024, tf=512)
        if l == depth - 1:
            (xf,) = _ffn_down(x1, act, w_ffn_down, final_norm, l, 0, True, tm=256)
        else:
            xf, h = _ffn_down(x1, act, w_ffn_down, attn_norm, l, l + 1, False, tm=256)
    return xf.reshape(b, s, d)
```
